```python
import math
import jax, jax.numpy as jnp
from jax import lax
import numpy as np

D_MODEL = 1024
BATCH = 4
SEQ = 4096
DEPTH = 2

GRID_W = 64
CTX_LEN = 256
Q_BLOCK = 128
ROPE_BASE = 10000.0
NORM_EPS = 1e-6

H_A = 8
N_A = 64
RW = H_A * N_A
DECAY_LORA = 64
ICLR_LORA = 64
GATE_LORA = 128
GN_EPS = 64e-5
RWKV_COLS = 3 * RW + 2 * DECAY_LORA + 2 * ICLR_LORA + GATE_LORA

H_B = 4
DIFF_HD = 64
DIFF_QK = H_B * 2 * DIFF_HD
DIFF_V = H_B * 2 * DIFF_HD
DIFF_COLS = 2 * DIFF_QK + DIFF_V

H_C = 8
Q_LORA = 384
KV_LORA = 256
MLA_NOPE = 64
MLA_ROPE = 32
MLA_V = 64
MLA_COLS = Q_LORA + KV_LORA + MLA_ROPE

N_BRANCH = 3
BRANCH_W = RW
GATE_COLS = N_BRANCH * D_MODEL
N_IN = RWKV_COLS + DIFF_COLS + MLA_COLS + GATE_COLS
D_FF = -(-8 * D_MODEL // (3 * 256)) * 256

kernel_name = 'hybrid_rwkv7_diffattn_mla_prefix_dit_block'


def rms_norm(x, g):
    xf = x.astype(jnp.float32)
    return (xf * lax.rsqrt(jnp.mean(xf * xf, axis=-1, keepdims=True) + NORM_EPS)).astype(x.dtype) * g


def modulate(h, shift, scale):
    return h * (1 + scale) + shift


def split_cols(z, sizes):
    idx, acc = [], 0
    for s in sizes[:-1]:
        acc += s
        idx.append(acc)
    return jnp.split(z, idx, axis=-1)


def axial_rope_tables(rows, rot_dim, dtype):
    row = jnp.repeat(jnp.arange(rows, dtype=jnp.float32), GRID_W)
    col = jnp.tile(jnp.arange(GRID_W, dtype=jnp.float32), rows)
    n_freq = rot_dim // 4
    inv_freq = ROPE_BASE ** (-jnp.arange(n_freq, dtype=jnp.float32) / n_freq)
    ang = jnp.concatenate([row[:, None] * inv_freq, col[:, None] * inv_freq], axis=-1)
    return jnp.cos(ang).astype(dtype), jnp.sin(ang).astype(dtype)


def apply_rope(x, cos, sin):
    x1, x2 = jnp.split(x, 2, axis=-1)
    return jnp.concatenate([x1 * cos - x2 * sin, x2 * cos + x1 * sin], axis=-1)


def softmax_probs(q, k, scale):
    s = jnp.einsum('bhqd,bhkd->bhqk', q, k).astype(jnp.float32) * scale
    return jax.nn.softmax(s, axis=-1)


def sweep_query_blocks(fn, q):
    B, H, T = q.shape[:3]
    nb = T // Q_BLOCK
    qb = jnp.moveaxis(q.reshape((B, H, nb, Q_BLOCK) + q.shape[3:]), 2, 0)
    out = lax.map(fn, qb)
    return jnp.moveaxis(out, 0, 2).reshape(B, H, T, out.shape[-1])


def centred_token_shift(z, mu_prev, mu_next):
    prev = jnp.pad(z[:, :-1], ((0, 0), (1, 0), (0, 0)))
    nxt = jnp.pad(z[:, 1:], ((0, 0), (0, 1), (0, 0)))
    return z + mu_prev * (prev - z) + mu_next * (nxt - z)


def rwkv7_prepare(z, mu, w0, w2, a0, a2, g2, k_k, k_a):
    B, T, _ = z.shape
    z = centred_token_shift(z, mu[0], mu[1])
    r, k, v, wd_f, wd_b, ad_f, ad_b, gd = split_cols(
        z, [RW, RW, RW, DECAY_LORA, DECAY_LORA, ICLR_LORA, ICLR_LORA, GATE_LORA])
    heads = lambda t: t.reshape(B, T, H_A, N_A).astype(jnp.float32)
    kk = heads(k * k_k)
    kk = kk / jnp.maximum(jnp.sqrt(jnp.sum(kk * kk, axis=-1, keepdims=True)), 1e-12)
    dirs = []
    for d, (wd, ad) in enumerate(((wd_f, ad_f), (wd_b, ad_b))):
        w_log = -jax.nn.softplus(-(w0[d] + jnp.tanh(wd) @ w2[d]).astype(jnp.float32)) - 0.5
        decay = jnp.exp(-jnp.exp(w_log))
        a = jax.nn.sigmoid(a0[d] + ad @ a2[d])
        k_dir = k * (1 + (a - 1) * k_a)
        dirs.append((heads(decay), heads(k_dir), kk * heads(a)))
    g = jax.nn.sigmoid(gd) @ g2
    return heads(r), heads(v), kk, g, dirs


def wkv7_scan(state0, r, v, kk, decay, k, b, reverse, emit):
    xs = tuple(jnp.moveaxis(t, 1, 0) for t in (r, decay, k, v, -kk, b))

    def step(S, inp):
        r_t, w_t, k_t, v_t, a_t, b_t = inp
        sa = jnp.einsum('bhvk,bhk->bhv', S, a_t)
        S = S * w_t[:, :, None, :] + sa[..., None] * b_t[:, :, None, :] + v_t[..., None] * k_t[:, :, None, :]
        return S, (jnp.einsum('bhvk,bhk->bhv', S, r_t) if emit else None)

    state, ys = lax.scan(step, state0, xs, reverse=reverse)
    return state, (jnp.moveaxis(ys, 0, 1) if emit else None)


def rwkv7_readout(ys, r, v, dirs, g, r_k, ln_w, ln_b):
    B, T = r.shape[:2]
    y = ys[0] + ys[1]
    mean = jnp.mean(y, axis=-1, keepdims=True)
    var = jnp.mean(jnp.square(y - mean), axis=-1, keepdims=True)
    y = (y - mean) * lax.rsqrt(var + GN_EPS)
    rk = r_k.astype(jnp.float32)
    bonus = (jnp.sum(r * dirs[0][1] * rk, axis=-1, keepdims=True)
             + jnp.sum(r * dirs[1][1] * rk, axis=-1, keepdims=True)) * v
    y = y.reshape(B, T, RW) * ln_w + ln_b + bonus.reshape(B, T, RW)
    return y.astype(g.dtype) * g


def rwkv7_branch(z_lat, z_ctx, mu, w0, w2, a0, a2, g2, k_k, k_a, r_k, ln_w, ln_b, emit_ctx):
    p = (mu, w0, w2, a0, a2, g2, k_k, k_a)
    r_l, v_l, kk_l, g_l, dirs_l = rwkv7_prepare(z_lat, *p)
    r_c, v_c, kk_c, g_c, dirs_c = rwkv7_prepare(z_ctx, *p)
    state0 = jnp.zeros((z_lat.shape[0], H_A, N_A, N_A), jnp.float32)
    ys_l, ys_c = [], []
    for d, reverse in enumerate((False, True)):
        state_c, y_c = wkv7_scan(state0, r_c, v_c, kk_c, *dirs_c[d], reverse, emit_ctx)
        _, y_l = wkv7_scan(state_c, r_l, v_l, kk_l, *dirs_l[d], reverse, True)
        ys_l.append(y_l)
        ys_c.append(y_c)
    out_l = rwkv7_readout(ys_l, r_l, v_l, dirs_l, g_l, r_k, ln_w, ln_b)
    out_c = rwkv7_readout(ys_c, r_c, v_c, dirs_c, g_c, r_k, ln_w, ln_b) if emit_ctx else None
    return out_l, out_c


def diff_attention_branch(z_lat, z_ctx, lam_vecs, subln_g, layer_idx, rope, emit_ctx):
    lambda_init = 0.8 - 0.6 * math.exp(-0.3 * layer_idx)
    lv = lam_vecs.astype(jnp.float32)
    lam = jnp.exp(jnp.sum(lv[0] * lv[1])) - jnp.exp(jnp.sum(lv[2] * lv[3])) + lambda_init
    scale = DIFF_HD ** -0.5

    def qkv(z):
        B, T, _ = z.shape
        q, k, v = split_cols(z, [DIFF_QK, DIFF_QK, DIFF_V])
        return (q.reshape(B, T, H_B, 2, DIFF_HD), k.reshape(B, T, H_B, 2, DIFF_HD),
                v.reshape(B, T, H_B, 2 * DIFF_HD))

    q_l, k_l, v_l = qkv(z_lat)
    q_c, k_c, v_c = qkv(z_ctx)
    cos, sin = rope[0][None, :, None, None, :], rope[1][None, :, None, None, :]
    q_l = apply_rope(q_l, cos, sin)
    k_l = apply_rope(k_l, cos, sin)
    bh = lambda t: jnp.moveaxis(t, 2, 1)
    k_all = bh(jnp.concatenate([k_c, k_l], axis=1))
    v_all = bh(jnp.concatenate([v_c, v_l], axis=1))

    def attend(q, k, v):
        p1 = softmax_probs(q[..., 0, :], k[..., 0, :], scale)
        p2 = softmax_probs(q[..., 1, :], k[..., 1, :], scale)
        return jnp.einsum('bhqk,bhkd->bhqd', (p1 - lam * p2).astype(v.dtype), v)

    def finish(o):
        o = rms_norm(o, subln_g) * (1.0 - lambda_init)
        B, H, T, E = o.shape
        return jnp.moveaxis(o, 1, 2).reshape(B, T, H * E)

    out_l = finish(sweep_query_blocks(lambda qb: attend(qb, k_all, v_all), bh(q_l)))
    out_c = finish(attend(bh(q_c), bh(k_c), bh(v_c))) if emit_ctx else None
    return out_l, out_c


def mla_branch(z_lat, z_ctx, q_norm_g, kv_norm_g, w_uq, w_uk, w_uv, rope, emit_ctx):
    scale = (MLA_NOPE + MLA_ROPE) ** -0.5

    def project(z, rope_tables):
        B, T, _ = z.shape
        cq, ckv, k_rope = split_cols(z, [Q_LORA, KV_LORA, MLA_ROPE])
        q = (rms_norm(cq, q_norm_g) @ w_uq).reshape(B, T, H_C, MLA_NOPE + MLA_ROPE)
        ckv = rms_norm(ckv, kv_norm_g)
        k_nope = (ckv @ w_uk).reshape(B, T, H_C, MLA_NOPE)
        v = (ckv @ w_uv).reshape(B, T, H_C, MLA_V)
        q_nope, q_rope = jnp.split(q, [MLA_NOPE], axis=-1)
        if rope_tables is not None:
            cos, sin = rope_tables
            q_rope = apply_rope(q_rope, cos[None, :, None, :], sin[None, :, None, :])
            k_rope = apply_rope(k_rope, cos[None], sin[None])
        k = jnp.concatenate([k_nope, jnp.broadcast_to(k_rope[:, :, None, :], (B, T, H_C, MLA_ROPE))], axis=-1)
        q = jnp.concatenate([q_nope, q_rope], axis=-1)
        return jnp.moveaxis(q, 2, 1), jnp.moveaxis(k, 2, 1), jnp.moveaxis(v, 2, 1)

    q_l, k_l, v_l = project(z_lat, rope)
    q_c, k_c, v_c = project(z_ctx, None)
    k_all = jnp.concatenate([k_c, k_l], axis=2)
    v_all = jnp.concatenate([v_c, v_l], axis=2)

    def attend(q, k, v):
        return jnp.einsum('bhqk,bhkd->bhqd', softmax_probs(q, k, scale).astype(v.dtype), v)

    def finish(o):
        B, H, T, E = o.shape
        return jnp.moveaxis(o, 1, 2).reshape(B, T, H * E)

    out_l = finish(sweep_query_blocks(lambda qb: attend(qb, k_all, v_all), q_l))
    out_c = finish(attend(q_c, k_c, v_c)) if emit_ctx else None
    return out_l, out_c


def merge_branches(ys, gate_logits, w_branch, w_out):
    B, T, _ = gate_logits.shape
    y = jnp.stack(ys, axis=2)
    proj = jnp.einsum('btge,ged->btgd', y, w_branch)
    gates = jax.nn.sigmoid(gate_logits.reshape(B, T, N_BRANCH, D_MODEL))
    return jnp.sum(gates * proj, axis=2) @ w_out


def swiglu(h, w_gate, w_up, w_down):
    return (jax.nn.silu(h @ w_gate) * (h @ w_up)) @ w_down


def setup_inputs(seed: int = 0) -> dict:
    key = jax.random.key(seed)
    keys = jax.random.split(key, 40)
    counter = [0]

    def nk():
        k = keys[counter[0]]
        counter[0] += 1
        return k

    nrm = lambda shape, s: jax.random.normal(nk(), shape, jnp.float32) * s
    uni = lambda shape, lo, hi: jax.random.uniform(nk(), shape, jnp.float32, lo, hi)
    L, D = DEPTH, D_MODEL
    return {
        'x': nrm((BATCH, SEQ, D), 1.0),
        'c': nrm((BATCH, D), 1.0),
        'ctx': nrm((BATCH, CTX_LEN, D), 1.0),
        'c_ctx': nrm((D,), 1.0),
        'ada_w': nrm((L, D, 6 * D), D ** -0.5),
        'ada_b': nrm((L, 6 * D), 0.02),
        'norm1_g': 1.0 + nrm((L, D), 0.02),
        'norm2_g': 1.0 + nrm((L, D), 0.02),
        'w_in': nrm((L, D, N_IN), D ** -0.5),
        'rwkv_shift_mu': uni((L, 2, RWKV_COLS), 0.0, 0.5),
        'rwkv_w0': uni((L, 2, RW), -6.0, 1.0),
        'rwkv_w2': nrm((L, 2, DECAY_LORA, RW), 0.5 * DECAY_LORA ** -0.5),
        'rwkv_a0': nrm((L, 2, RW), 0.1),
        'rwkv_a2': nrm((L, 2, ICLR_LORA, RW), 0.5 * ICLR_LORA ** -0.5),
        'rwkv_g2': nrm((L, GATE_LORA, RW), GATE_LORA ** -0.5),
        'rwkv_k_k': 0.85 + nrm((L, RW), 0.02),
        'rwkv_k_a': 1.0 + nrm((L, RW), 0.02),
        'rwkv_r_k': -0.04 + nrm((L, H_A, N_A), 0.02),
        'rwkv_ln_w': 1.0 + nrm((L, RW), 0.02),
        'rwkv_ln_b': nrm((L, RW), 0.02),
        'diff_lambda': nrm((L, 4, DIFF_HD), 0.1),
        'diff_subln_g': 1.0 + nrm((L, 2 * DIFF_HD), 0.02),
        'mla_q_norm_g': 1.0 + nrm((L, Q_LORA), 0.02),
        'mla_kv_norm_g': 1.0 + nrm((L, KV_LORA), 0.02),
        'mla_w_uq': nrm((L, Q_LORA, H_C * (MLA_NOPE + MLA_ROPE)), Q_LORA ** -0.5),
        'mla_w_uk': nrm((L, KV_LORA, H_C * MLA_NOPE), KV_LORA ** -0.5),
        'mla_w_uv': nrm((L, KV_LORA, H_C * MLA_V), KV_LORA ** -0.5),
        'w_branch': nrm((L, N_BRANCH, BRANCH_W, D), BRANCH_W ** -0.5),
        'w_out': nrm((L, D, D), D ** -0.5),
        'ffn_w_gate': nrm((L, D, D_FF), D ** -0.5),
        'ffn_w_up': nrm((L, D, D_FF), D ** -0.5),
        'ffn_w_down': nrm((L, D_FF, D), D_FF ** -0.5),
        'final_norm_g': 1.0 + nrm((D,), 0.02),
    }


def reference(x, c, ctx, c_ctx, ada_w, ada_b, norm1_g, norm2_g, w_in, rwkv_shift_mu, rwkv_w0, rwkv_w2,
              rwkv_a0, rwkv_a2, rwkv_g2, rwkv_k_k, rwkv_k_a, rwkv_r_k, rwkv_ln_w, rwkv_ln_b, diff_lambda,
              diff_subln_g, mla_q_norm_g, mla_kv_norm_g, mla_w_uq, mla_w_uk, mla_w_uv, w_branch, w_out,
              ffn_w_gate, ffn_w_up, ffn_w_down, final_norm_g):
    B, T, D = x.shape
    rows = T // GRID_W
    rope_b = axial_rope_tables(rows, DIFF_HD, x.dtype)
    rope_c = axial_rope_tables(rows, MLA_ROPE, x.dtype)
    col_sizes = [RWKV_COLS, DIFF_COLS, MLA_COLS, GATE_COLS]
    sc = jax.nn.silu(c)
    scc = jax.nn.silu(c_ctx)
    x_lat, x_ctx = x, ctx
    for l in range(DEPTH):
        emit_ctx = l < DEPTH - 1
        sh1, s1, g1, sh2, s2, g2 = [m[:, None, :] for m in jnp.split(sc @ ada_w[l] + ada_b[l], 6, axis=-1)]
        csh1, cs1, cg1, csh2, cs2, cg2 = jnp.split(scc @ ada_w[l] + ada_b[l], 6, axis=-1)

        h_lat = modulate(rms_norm(x_lat, norm1_g[l]), sh1, s1)
        h_ctx = modulate(rms_norm(x_ctx, norm1_g[l]), csh1, cs1)
        za, zb, zc, zg = split_cols(h_lat @ w_in[l], col_sizes)
        za_c, zb_c, zc_c, zg_c = split_cols(h_ctx @ w_in[l], col_sizes)
        ya, ya_c = rwkv7_branch(za, za_c, rwkv_shift_mu[l], rwkv_w0[l], rwkv_w2[l], rwkv_a0[l], rwkv_a2[l],
                                rwkv_g2[l], rwkv_k_k[l], rwkv_k_a[l], rwkv_r_k[l], rwkv_ln_w[l], rwkv_ln_b[l],
                                emit_ctx)
        yb, yb_c = diff_attention_branch(zb, zb_c, diff_lambda[l], diff_subln_g[l], l, rope_b, emit_ctx)
        yc, yc_c = mla_branch(zc, zc_c, mla_q_norm_g[l], mla_kv_norm_g[l], mla_w_uq[l], mla_w_uk[l],
                              mla_w_uv[l], rope_c, emit_ctx)
        x_lat = x_lat + g1 * merge_branches((ya, yb, yc), zg, w_branch[l], w_out[l])

        h2 = modulate(rms_norm(x_lat, norm2_g[l]), sh2, s2)
        x_lat = x_lat + g2 * swiglu(h2, ffn_w_gate[l], ffn_w_up[l], ffn_w_down[l])

        if emit_ctx:
            x_ctx = x_ctx + cg1 * merge_branches((ya_c, yb_c, yc_c), zg_c, w_branch[l], w_out[l])
            h2c = modulate(rms_norm(x_ctx, norm2_g[l]), csh2, cs2)
            x_ctx = x_ctx + cg2 * swiglu(h2c, ffn_w_gate[l], ffn_w_up[l], ffn_w_down[l])
    return rms_norm(x_lat, final_norm_g)
```

```python
import functools
import math

import jax
import jax.numpy as jnp
from jax import lax
from jax.experimental import pallas as pl
from jax.experimental.pallas import tpu as pltpu

F32 = jnp.float32
BF16 = jnp.bfloat16

D = 1024
CTX = 256
GRID_W = 64
ROPE_BASE = 10000.0
NORM_EPS = 1e-6
GN_EPS = 64e-5

H_A, N_A = 8, 64
RW = H_A * N_A
RWKV_COLS = 3 * RW + 2 * 64 + 2 * 64 + 128
H_B, DIFF_HD = 4, 64
DIFF_QK = H_B * 2 * DIFF_HD
DIFF_COLS = 3 * DIFF_QK
H_C = 8
Q_LORA, KV_LORA = 384, 256
MLA_NOPE, MLA_ROPE, MLA_V = 64, 32, 64
MLA_COLS = Q_LORA + KV_LORA + MLA_ROPE
MLA_PAD = 896
D_FF = 2816

TM = 256
LANE = 128
CHUNK = 64
TK = 256
LOG2E = 1.4426950408889634
VMEM_LIMIT = 56 * 1024 * 1024


def _cparams(sem):
    return pltpu.CompilerParams(dimension_semantics=sem, vmem_limit_bytes=VMEM_LIMIT)


def _sigmoid(x):
    return 1.0 / (1.0 + jnp.exp(-x))


def _dot(a, b):
    return jnp.dot(a, b, preferred_element_type=F32)


def _dot_nt(a, b):
    return lax.dot_general(a, b, (((1,), (1,)), ((), ())), preferred_element_type=F32)


def _dot_tn(a, b):
    return lax.dot_general(a, b, (((0,), (0,)), ((), ())), preferred_element_type=F32)


def _seg_sum(x, ones_bd):
    hi = x.astype(BF16)
    lo = (x - hi.astype(F32)).astype(BF16)
    return _dot(hi, ones_bd) + _dot(lo, ones_bd)


def _is_ctx(i):
    return jnp.where(i == 0, 1, 0)


def _mod_kernel(c_ref, w_ref, b_ref, o_ref):
    c = c_ref[...]
    s = (c * _sigmoid(c)).astype(BF16)
    o_ref[0] = _dot(s, w_ref[0]) + b_ref[0]


def _modulation(c8, ada_w, ada_b):
    L = ada_w.shape[0]
    tn = 1536
    return pl.pallas_call(
        _mod_kernel,
        grid=(L, 6 * D // tn),
        in_specs=[
            pl.BlockSpec((8, D), lambda l, j: (0, 0)),
            pl.BlockSpec((1, D, tn), lambda l, j: (l, 0, j)),
            pl.BlockSpec((1, 1, tn), lambda l, j: (l, 0, j)),
        ],
        out_specs=pl.BlockSpec((1, 8, tn), lambda l, j: (l, 0, j)),
        out_shape=jax.ShapeDtypeStruct((L, 8, 6 * D), F32),
        compiler_params=_cparams(("parallel", "parallel")),
        name="adaln_mod",
    )(c8, ada_w.astype(BF16), ada_b[:, None, :])


def _norm_mod(x, g, shift, scale):
    ms = jnp.mean(x * x, axis=-1, keepdims=True)
    return (x * lax.rsqrt(ms + NORM_EPS)) * g * (1.0 + scale) + shift


def _inproj_kernel(x_ref, g_ref, mod_ref, w_ref, o_ref):
    h = _norm_mod(x_ref[0], g_ref[...], mod_ref[0, 0, 0:1, :], mod_ref[0, 0, 1:2, :]).astype(BF16)
    o_ref[0] = _dot(h, w_ref[...])


def _inproj_rope_kernel(n_rope, x_ref, g_ref, mod_ref, w_ref, wrot_ref, cos_ref, sin_ref, o_ref):
    h = _norm_mod(x_ref[0], g_ref[...], mod_ref[0, 0, 0:1, :], mod_ref[0, 0, 1:2, :]).astype(BF16)
    z = _dot(h, w_ref[...])
    zr = _dot(h, wrot_ref[...])
    cos = cos_ref[...]
    sin = sin_ref[...]
    for j in range(n_rope // LANE):
        sl = slice(j * LANE, (j + 1) * LANE)
        o_ref[0, :, sl] = z[:, sl] * cos + zr[:, sl] * sin
    o_ref[0, :, n_rope:] = z[:, n_rope:]


def _inproj(x_all, g, mod, w, rope=None):
    B, T, _ = x_all.shape
    N = w.shape[1]
    specs = [
        pl.BlockSpec((1, TM, D), lambda b, i: (b, i, 0)),
        pl.BlockSpec((1, D), lambda b, i: (0, 0)),
        pl.BlockSpec((1, 1, 2, D), lambda b, i: (b, _is_ctx(i), 0, 0)),
        pl.BlockSpec((D, N), lambda b, i: (0, 0)),
    ]
    args = [x_all, g, mod, w]
    if rope is None:
        body = _inproj_kernel
    else:
        wrot, cos, sin = rope
        n_rope = wrot.shape[1]
        body = functools.partial(_inproj_rope_kernel, n_rope)
        specs += [
            pl.BlockSpec((D, n_rope), lambda b, i: (0, 0)),
            pl.BlockSpec((TM, LANE), lambda b, i: (i, 0)),
            pl.BlockSpec((TM, LANE), lambda b, i: (i, 0)),
        ]
        args += [wrot, cos, sin]
    return pl.pallas_call(
        body,
        grid=(B, T // TM),
        in_specs=specs,
        out_specs=pl.BlockSpec((1, TM, N), lambda b, i: (b, i, 0)),
        out_shape=jax.ShapeDtypeStruct((B, T, N), F32),
        compiler_params=_cparams(("parallel", "parallel")),
        name="inproj",
    )(*args)


def _rwkv_prep_kernel(z_ref, hp_ref, hn_ref, mu_ref, w0_ref, w2_ref, a0_ref, a2_ref, g2_ref, kk_ref, ka_ref,
                      rk_ref, ones_ref,
                      r_o, v_o, a_o, lwf_o, lwb_o, kf_o, kb_o, bf_o, bb_o, g_o, bonus_o):
    z = z_ref[0]
    row = lax.broadcasted_iota(jnp.int32, z.shape, 0)
    prev = jnp.where(row == 0, hp_ref[0, 0], pltpu.roll(z, 1, 0))
    nxt = jnp.where(row == TM - 1, hn_ref[0, 0], pltpu.roll(z, TM - 1, 0))
    zs = z + mu_ref[0:1, :] * (prev - z) + mu_ref[1:2, :] * (nxt - z)
    r = zs[:, 0:RW]
    k = zs[:, RW:2 * RW]
    v = zs[:, 2 * RW:3 * RW]
    wd = zs[:, 3 * RW:3 * RW + LANE]
    ad = zs[:, 3 * RW + LANE:3 * RW + 2 * LANE]
    gd = zs[:, 3 * RW + 2 * LANE:3 * RW + 3 * LANE]
    ones_bd = ones_ref[...]
    kk = k * kk_ref[...]
    nrm = jnp.sqrt(_seg_sum(kk * kk, ones_bd))
    kk = kk / jnp.maximum(nrm, 1e-12)
    tw = jnp.tanh(wd).astype(BF16)
    adb = ad.astype(BF16)
    rrk = r * rk_ref[...]
    bonus = jnp.zeros_like(r)
    outs = ((lwf_o, kf_o, bf_o), (lwb_o, kb_o, bb_o))
    for d in range(2):
        u = w0_ref[d:d + 1, :] + _dot(tw, w2_ref[d])
        lw = -math.exp(-0.5) * _sigmoid(u)
        alr = _sigmoid(a0_ref[d:d + 1, :] + _dot(adb, a2_ref[d]))
        kd = k * (1.0 + (alr - 1.0) * ka_ref[...])
        lw_o, k_o, b_o = outs[d]
        lw_o[0] = lw
        k_o[0] = kd
        b_o[0] = kk * alr
        bonus = bonus + _seg_sum(rrk * kd, ones_bd)
    r_o[0] = r
    v_o[0] = v
    a_o[0] = -kk
    g_o[0] = _dot(_sigmoid(gd).astype(BF16), g2_ref[...])
    bonus_o[0] = bonus * v


def _rwkv_prepare(za, hprev, hnext, p):
    B, T, _ = za.shape
    nt = T // TM
    full = lambda a: pl.BlockSpec(a.shape, lambda b, i: (0,) * a.ndim)
    params = [p["mu"], p["w0"], p["w2"], p["a0"], p["a2"], p["g2"], p["k_k"], p["k_a"], p["r_k"], p["ones_bd"]]
    out_spec = pl.BlockSpec((1, TM, RW), lambda b, i: (b, i, 0))
    return pl.pallas_call(
        _rwkv_prep_kernel,
        grid=(B, nt),
        in_specs=[
            pl.BlockSpec((1, TM, RWKV_COLS), lambda b, i: (b, i, 0)),
            pl.BlockSpec((1, 1, 1, RWKV_COLS), lambda b, i: (b, i, 0, 0)),
            pl.BlockSpec((1, 1, 1, RWKV_COLS), lambda b, i: (b, i, 0, 0)),
        ] + [full(a) for a in params],
        out_specs=[out_spec] * 11,
        out_shape=[jax.ShapeDtypeStruct((B, T, RW), F32)] * 11,
        compiler_params=_cparams(("parallel", "parallel")),
        name="rwkv_prepare",
    )(za, hprev, hnext, *params)


def _wkv_consts(rev):
    t = lax.broadcasted_iota(jnp.int32, (CHUNK, LANE), 0)
    lane = lax.broadcasted_iota(jnp.int32, (CHUNK, LANE), 1)
    s = jnp.bitwise_and(lane, CHUNK - 1)
    tt = lax.broadcasted_iota(jnp.int32, (CHUNK, CHUNK), 0)
    ss = lax.broadcasted_iota(jnp.int32, (CHUNK, CHUNK), 1)
    r2 = lax.broadcasted_iota(jnp.int32, (LANE, LANE), 0)
    l2 = lax.broadcasted_iota(jnp.int32, (LANE, LANE), 1)
    return dict(
        strict=(s > t) if rev else (s < t),
        incl=(s >= t) if rev else (s <= t),
        eye=jnp.where(s == t, 1.0, 0.0).astype(F32),
        m0=lane < CHUNK,
        tri=jnp.where((ss >= tt) if rev else (ss <= tt), 1.0, 0.0).astype(F32),
        bdm=(r2 >= CHUNK) == (l2 >= CHUNK),
        ones=jnp.ones((LANE, LANE), F32),
        levels=[((t >> (j + 1)) == (s >> (j + 1))) & ((t >> j) != (s >> j)) for j in range(int(math.log2(CHUNK)))],
    )


def _bd(y, m0):
    z = jnp.zeros_like(y)
    return jnp.concatenate([jnp.where(m0, y, z), jnp.where(m0, z, y)], axis=0)


def _wkv_chunk(H, r, lw, k, v, a, b, cs, rev):
    C = CHUNK
    m0 = cs["m0"]
    zero = jnp.zeros((C, LANE), F32)
    lw_hi = lw.astype(BF16).astype(F32)
    lw_lo = lw - lw_hi
    cc = _dot(cs["tri"], jnp.concatenate([lw_hi, lw_lo], axis=1))
    cum = cc[:, :LANE] + cc[:, LANE:]
    total = cum[0:1] if rev else cum[C - 1:C]
    pcol = jnp.exp(_dot_tn(jnp.concatenate([lw_hi, lw_lo], axis=0), cs["ones"]))
    p_inc = jnp.exp(cum)
    p_prev = jnp.exp(cum - lw)
    p_inv = jnp.exp(-cum)
    p_sc = jnp.exp(total - cum)
    rt, at, bt, kt, bs, ks = r * p_inc, a * p_prev, b * p_inv, k * p_inv, b * p_sc, k * p_sc
    pw = _dot_nt(jnp.concatenate([at, rt], axis=0), jnp.concatenate([_bd(bt, m0), _bd(kt, m0)], axis=0))
    ab = jnp.where(cs["strict"], pw[:C, :LANE], zero)
    ak = jnp.where(cs["strict"], pw[:C, LANE:], zero)
    rb = jnp.where(cs["incl"], pw[C:, :LANE], zero)
    rk = jnp.where(cs["incl"], pw[C:, LANE:], zero)
    tinv = cs["eye"] + jnp.where(cs["levels"][0], ab, zero)
    for lvl in cs["levels"][1:]:
        tl = _dot(tinv, _bd(jnp.where(lvl, ab, zero), m0))
        tinv = tinv + _dot(tl, _bd(tinv, m0))
    akv = _dot(jnp.concatenate([ak, rk], axis=0), _bd(v, m0))
    wx = _dot(tinv, jnp.concatenate([_bd(at, m0), _bd(akv[:C], m0)], axis=1))
    s1 = _dot(jnp.concatenate([wx[:, :LANE], rt], axis=0), H)
    u = s1[:C] + wx[:, LANE:]
    y = s1[C:] + _dot(rb, _bd(u, m0)) + akv[C:]
    s3 = _dot_tn(jnp.concatenate([bs, ks], axis=0), jnp.concatenate([u, v], axis=0))
    h_new = pcol * H + jnp.where(cs["bdm"], s3, jnp.zeros_like(s3))
    return h_new, y


def _wkv_kernel(rf, vf, af, lwf, kf, bf, rb, vb, ab, lwb, kb, bb, yf_o, yb_o, hf_s, hb_s):
    @pl.when(pl.program_id(2) == 0)
    def _():
        hf_s[...] = jnp.zeros_like(hf_s)
        hb_s[...] = jnp.zeros_like(hb_s)

    nch = TM // CHUNK
    for rev, refs, y_o, h_s in ((False, (rf, lwf, kf, vf, af, bf), yf_o, hf_s),
                                (True, (rb, lwb, kb, vb, ab, bb), yb_o, hb_s)):
        cs = _wkv_consts(rev)
        H = h_s[...]
        for c in (range(nch - 1, -1, -1) if rev else range(nch)):
            sl = slice(c * CHUNK, (c + 1) * CHUNK)
            H, y = _wkv_chunk(H, *[ref[0, sl, :] for ref in refs], cs, rev)
            y_o[0, sl, :] = y
        h_s[...] = H


def _wkv_scan(r, v, a, lwf, lwb, kf, kb, bf, bb):
    B, T, _ = r.shape
    nt = T // TM
    fwd = pl.BlockSpec((1, TM, LANE), lambda b, p, i: (b, i, p))
    bwd = pl.BlockSpec((1, TM, LANE), lambda b, p, i: (b, jnp.where(i == 0, 0, nt - i), p))
    return pl.pallas_call(
        _wkv_kernel,
        grid=(B, RW // LANE, nt),
        in_specs=[fwd] * 6 + [bwd] * 6,
        out_specs=[fwd, bwd],
        out_shape=[jax.ShapeDtypeStruct((B, T, RW), F32)] * 2,
        scratch_shapes=[pltpu.VMEM((LANE, LANE), F32), pltpu.VMEM((LANE, LANE), F32)],
        compiler_params=_cparams(("parallel", "parallel", "arbitrary")),
        name="wkv_scan",
    )(r, v, a, lwf, kf, bf, r, v, a, lwb, kb, bb)


def _load_vt(v_ref, vt_s):
    for c in range(v_ref.shape[1] // TK):
        vt_s[c] = v_ref[0, c * TK:(c + 1) * TK, :].T


def _online_update(s, vt, m, l, acc):
    m_new = jnp.maximum(m, jnp.max(s, axis=0, keepdims=True))
    p = jnp.exp2(s - m_new)
    alpha = jnp.exp2(m - m_new)
    l_new = alpha * l + jnp.sum(p, axis=0, keepdims=True)
    acc_new = alpha * acc + _dot(vt, p)
    return m_new, l_new, acc_new


def _diff_attn_kernel(lam_init, q_ref, k_ref, v_ref, lamv_ref, g_ref, o_ref, vt_s):
    i = pl.program_id(2)

    @pl.when(i == 0)
    def _():
        _load_vt(v_ref, vt_s)

    lv = lamv_ref[...]
    lam = (jnp.exp(jnp.sum(lv[0:1] * lv[1:2], axis=-1, keepdims=True))
           - jnp.exp(jnp.sum(lv[2:3] * lv[3:4], axis=-1, keepdims=True)) + lam_init)
    q = q_ref[0] * (DIFF_HD ** -0.5 * LOG2E)
    lane = lax.broadcasted_iota(jnp.int32, q.shape, 1)
    zq = jnp.zeros_like(q)
    q1 = jnp.where(lane < DIFF_HD, q, zq)
    q2 = jnp.where(lane < DIFF_HD, zq, q)
    n_chunks = jnp.where(i == 0, 1, k_ref.shape[1] // TK)

    def body(c, carry):
        m1, l1, a1, m2, l2, a2 = carry
        kc = k_ref[0, pl.ds(pl.multiple_of(c * TK, TK), TK), :]
        vt = vt_s[c]
        m1, l1, a1 = _online_update(_dot_nt(kc, q1), vt, m1, l1, a1)
        m2, l2, a2 = _online_update(_dot_nt(kc, q2), vt, m2, l2, a2)
        return m1, l1, a1, m2, l2, a2

    neg = jnp.full((1, TM), -1e30, F32)
    zl = jnp.zeros((1, TM), F32)
    za = jnp.zeros((2 * DIFF_HD, TM), F32)
    m1, l1, a1, m2, l2, a2 = lax.fori_loop(0, n_chunks, body, (neg, zl, za, neg, zl, za))
    o = a1 / l1 - lam * (a2 / l2)
    ms = jnp.mean(o * o, axis=0, keepdims=True)
    o = o * lax.rsqrt(ms + NORM_EPS)
    o_ref[0] = o.T * g_ref[...] * (1.0 - lam_init)


def _diff_attention(zb, lam_vecs, subln_g, layer_idx):
    B, T, _ = zb.shape
    nt = T // TM
    lam_init = 0.8 - 0.6 * math.exp(-0.3 * layer_idx)
    return pl.pallas_call(
        functools.partial(_diff_attn_kernel, lam_init),
        grid=(B, H_B, nt),
        in_specs=[
            pl.BlockSpec((1, TM, LANE), lambda b, h, i: (b, i, h)),
            pl.BlockSpec((1, T, LANE), lambda b, h, i: (b, 0, H_B + h)),
            pl.BlockSpec((1, T, LANE), lambda b, h, i: (b, 0, 2 * H_B + h)),
            pl.BlockSpec((4, DIFF_HD), lambda b, h, i: (0, 0)),
            pl.BlockSpec((1, LANE), lambda b, h, i: (0, 0)),
        ],
        out_specs=pl.BlockSpec((1, TM, LANE), lambda b, h, i: (b, i, h)),
        out_shape=jax.ShapeDtypeStruct((B, T, H_B * LANE), F32),
        scratch_shapes=[pltpu.VMEM((T // TK, LANE, TK), F32)],
        compiler_params=_cparams(("parallel", "parallel", "arbitrary")),
        name="diff_attn",
    )(zb, zb, zb, lam_vecs, subln_g)


def _mla_proj_kernel(z_ref, qg_ref, kvg_ref, wq_ref, wqr_ref, wk_ref, wv_ref, cq_ref, sq_ref, ck_ref, sk_ref,
                     q_o, k_o, v_o):
    z = z_ref[0]
    cq = z[:, 0:Q_LORA]
    ckv = z[:, Q_LORA:Q_LORA + KV_LORA]
    kr = z[:, 5 * LANE:6 * LANE] * ck_ref[...] + z[:, 6 * LANE:7 * LANE] * sk_ref[...]
    rms = lambda t: t * lax.rsqrt(jnp.mean(t * t, axis=-1, keepdims=True) + NORM_EPS)
    cqn = (rms(cq) * qg_ref[...]).astype(BF16)
    ckvn = rms(ckv) * kvg_ref[...]
    q = _dot(cqn, wq_ref[...])
    qr = _dot(cqn, wqr_ref[...])
    cos = cq_ref[...]
    sin = sq_ref[...]
    scale = (MLA_NOPE + MLA_ROPE) ** -0.5 * LOG2E
    for h in range(H_C):
        sl = slice(h * LANE, (h + 1) * LANE)
        q_o[0, :, sl] = (q[:, sl] * cos + qr[:, sl] * sin) * scale
    kin = jnp.concatenate([ckvn, kr], axis=1).astype(BF16)
    k_o[0] = _dot(kin, wk_ref[...])
    v_o[0] = _dot(ckvn.astype(BF16), wv_ref[...])


def _mla_project(zc, p):
    B, T, _ = zc.shape
    full = lambda a: pl.BlockSpec(a.shape, lambda b, i: (0,) * a.ndim)
    tab = pl.BlockSpec((TM, LANE), lambda b, i: (i, 0))
    ws = [p["q_g"], p["kv_g"], p["w_uq"], p["w_uq_rot"], p["w_uk"], p["w_uv"]]
    return pl.pallas_call(
        _mla_proj_kernel,
        grid=(B, T // TM),
        in_specs=[pl.BlockSpec((1, TM, MLA_PAD), lambda b, i: (b, i, 0))] + [full(a) for a in ws] + [tab] * 4,
        out_specs=[
            pl.BlockSpec((1, TM, H_C * LANE), lambda b, i: (b, i, 0)),
            pl.BlockSpec((1, TM, H_C * LANE), lambda b, i: (b, i, 0)),
            pl.BlockSpec((1, TM, H_C * MLA_V), lambda b, i: (b, i, 0)),
        ],
        out_shape=[
            jax.ShapeDtypeStruct((B, T, H_C * LANE), F32),
            jax.ShapeDtypeStruct((B, T, H_C * LANE), F32),
            jax.ShapeDtypeStruct((B, T, H_C * MLA_V), F32),
        ],
        compiler_params=_cparams(("parallel", "parallel")),
        name="mla_project",
    )(zc, *ws, p["cos_q"], p["sin_q"], p["cos_k"], p["sin_k"])


def _mla_attn_kernel(q_ref, k_ref, v_ref, o_ref, vt_s):
    i = pl.program_id(2)

    @pl.when(i == 0)
    def _():
        _load_vt(v_ref, vt_s)

    q = q_ref[0]
    q0 = q[:, :LANE]
    q1 = q[:, LANE:]
    n_chunks = jnp.where(i == 0, 1, k_ref.shape[1] // TK)

    def body(c, carry):
        m0, l0, a0, m1, l1, a1 = carry
        kc = k_ref[0, pl.ds(pl.multiple_of(c * TK, TK), TK), :]
        vt = vt_s[c]
        m0, l0, a0 = _online_update(_dot_nt(kc[:, :LANE], q0), vt[:MLA_V], m0, l0, a0)
        m1, l1, a1 = _online_update(_dot_nt(kc[:, LANE:], q1), vt[MLA_V:], m1, l1, a1)
        return m0, l0, a0, m1, l1, a1

    neg = jnp.full((1, TM), -1e30, F32)
    zl = jnp.zeros((1, TM), F32)
    za = jnp.zeros((MLA_V, TM), F32)
    m0, l0, a0, m1, l1, a1 = lax.fori_loop(0, n_chunks, body, (neg, zl, za, neg, zl, za))
    o_ref[0] = jnp.concatenate([a0 / l0, a1 / l1], axis=0).T


def _mla_attention(q, k, v):
    B, T, _ = q.shape
    nt = T // TM
    return pl.pallas_call(
        _mla_attn_kernel,
        grid=(B, H_C // 2, nt),
        in_specs=[
            pl.BlockSpec((1, TM, 2 * LANE), lambda b, p, i: (b, i, p)),
            pl.BlockSpec((1, T, 2 * LANE), lambda b, p, i: (b, 0, p)),
            pl.BlockSpec((1, T, LANE), lambda b, p, i: (b, 0, p)),
        ],
        out_specs=pl.BlockSpec((1, TM, LANE), lambda b, p, i: (b, i, p)),
        out_shape=jax.ShapeDtypeStruct((B, T, H_C * MLA_V), F32),
        scratch_shapes=[pltpu.VMEM((T // TK, LANE, TK), F32)],
        compiler_params=_cparams(("parallel", "parallel", "arbitrary")),
        name="mla_attn",
    )(q, k, v)


def _merge_kernel(x_ref, yf_ref, yb_ref, bonus_ref, g_ref, yd_ref, yc_ref, zg_ref, g1_ref, lnw_ref, lnb_ref,
                  ones_ref, wb_ref, wo_ref, o_ref):
    ones_bd = ones_ref[...]
    y = yf_ref[0] + yb_ref[0]
    mean = _seg_sum(y, ones_bd) * (1.0 / N_A)
    yc = y - mean
    var = _seg_sum(yc * yc, ones_bd) * (1.0 / N_A)
    ya = (yc * lax.rsqrt(var + GN_EPS) * lnw_ref[...] + lnb_ref[...] + bonus_ref[0]) * g_ref[0]
    acc = None
    for j, yj in enumerate((ya, yd_ref[0], yc_ref[0])):
        gate = _sigmoid(zg_ref[0, :, j * D:(j + 1) * D])
        t = gate * _dot(yj.astype(BF16), wb_ref[j])
        acc = t if acc is None else acc + t
    o_ref[0] = x_ref[0] + g1_ref[0, 0] * _dot(acc.astype(BF16), wo_ref[...])


def _merge(x_all, yf, yb, bonus, g, yd, yc, zg, g1, lnw, lnb, ones_bd, wb, wo):
    B, T, _ = x_all.shape
    row = lambda n: pl.BlockSpec((1, TM, n), lambda b, i: (b, i, 0))
    full = lambda a: pl.BlockSpec(a.shape, lambda b, i: (0,) * a.ndim)
    return pl.pallas_call(
        _merge_kernel,
        grid=(B, T // TM),
        in_specs=[row(D)] + [row(RW)] * 6 + [row(3 * D),
                  pl.BlockSpec((1, 1, 1, D), lambda b, i: (b, _is_ctx(i), 0, 0)),
                  full(lnw), full(lnb), full(ones_bd), full(wb), full(wo)],
        out_specs=row(D),
        out_shape=jax.ShapeDtypeStruct((B, T, D), F32),
        compiler_params=_cparams(("parallel", "parallel")),
        name="merge",
    )(x_all, yf, yb, bonus, g, yd, yc, zg, g1, lnw, lnb, ones_bd, wb, wo)


def _ffn_kernel(final, x_ref, g_ref, mod_ref, wg_ref, wu_ref, wd_ref, fg_ref, o_ref):
    x = x_ref[0]
    h = _norm_mod(x, g_ref[...], mod_ref[0, 0, 0:1, :], mod_ref[0, 0, 1:2, :]).astype(BF16)
    gate = _dot(h, wg_ref[...])
    up = _dot(h, wu_ref[...])
    act = (gate * _sigmoid(gate) * up).astype(BF16)
    out = x + mod_ref[0, 0, 2:3, :] * _dot(act, wd_ref[...])
    if final:
        ms = jnp.mean(out * out, axis=-1, keepdims=True)
        out = out * lax.rsqrt(ms + NORM_EPS) * fg_ref[...]
    o_ref[0] = out


def _ffn(x_all, g, mod, wg, wu, wd, fg, final):
    B, T, _ = x_all.shape
    full = lambda a: pl.BlockSpec(a.shape, lambda b, i: (0,) * a.ndim, pipeline_mode=pl.Buffered(1))
    return pl.pallas_call(
        functools.partial(_ffn_kernel, final),
        grid=(B, T // TM),
        in_specs=[
            pl.BlockSpec((1, TM, D), lambda b, i: (b, i, 0)),
            pl.BlockSpec((1, D), lambda b, i: (0, 0)),
            pl.BlockSpec((1, 1, 3, D), lambda b, i: (b, _is_ctx(i), 0, 0)),
            full(wg), full(wu), full(wd),
            pl.BlockSpec((1, D), lambda b, i: (0, 0)),
        ],
        out_specs=pl.BlockSpec((1, TM, D), lambda b, i: (b, i, 0)),
        out_shape=jax.ShapeDtypeStruct((B, T, D), F32),
        compiler_params=_cparams(("parallel", "parallel")),
        name="ffn",
    )(x_all, g, mod, wg, wu, wd, fg)


def _rot_cols(w, half):
    d, n = w.shape
    w4 = w.reshape(d, n // (2 * half), 2, half)
    return jnp.concatenate([-w4[:, :, 1:2], w4[:, :, 0:1]], axis=2).reshape(d, n)


def _rope_tables(n_lat):
    rows = n_lat // GRID_W
    row = jnp.repeat(jnp.arange(rows, dtype=F32), GRID_W)
    col = jnp.tile(jnp.arange(GRID_W, dtype=F32), rows)

    def tables(rot_dim):
        n_freq = rot_dim // 4
        inv_freq = ROPE_BASE ** (-jnp.arange(n_freq, dtype=F32) / n_freq)
        ang = jnp.concatenate([row[:, None] * inv_freq, col[:, None] * inv_freq], axis=-1)
        cos = jnp.concatenate([jnp.ones((CTX, rot_dim // 2), F32), jnp.cos(ang)], axis=0)
        sin = jnp.concatenate([jnp.zeros((CTX, rot_dim // 2), F32), jnp.sin(ang)], axis=0)
        return cos, sin

    T = CTX + n_lat
    cos_b, sin_b = tables(DIFF_HD)
    cos_c, sin_c = tables(MLA_ROPE)
    one = lambda n: jnp.ones((T, n), F32)
    zero = lambda n: jnp.zeros((T, n), F32)
    return dict(
        cos_b=jnp.tile(cos_b, (1, 4)), sin_b=jnp.tile(sin_b, (1, 4)),
        cos_q=jnp.concatenate([one(MLA_NOPE), cos_c, cos_c, one(32)], axis=1),
        sin_q=jnp.concatenate([zero(MLA_NOPE), sin_c, sin_c, zero(32)], axis=1),
        cos_k=jnp.concatenate([cos_c, cos_c, one(96)], axis=1),
        sin_k=jnp.concatenate([sin_c, sin_c, zero(96)], axis=1),
    )


def _pad_heads(w, per_head, n_heads):
    d = w.shape[0]
    w3 = w.reshape(d, n_heads, per_head)
    return jnp.pad(w3, ((0, 0), (0, 0), (0, LANE - per_head))).reshape(d, n_heads * LANE)


def _layer_weights(l, ws):
    (w_in, mu, w0, w2, a0, a2, g2, k_k, k_a, r_k, ln_w, ln_b, q_g, kv_g, w_uq, w_uk, w_uv, w_branch, w_out,
     wg, wu, wd) = [w[l] for w in ws]
    o1, o2, o3 = RWKV_COLS, RWKV_COLS + DIFF_COLS, RWKV_COLS + DIFF_COLS + MLA_COLS
    w_a, w_b, w_c, w_g = w_in[:, :o1], w_in[:, o1:o2], w_in[:, o2:o3], w_in[:, o3:]
    w_kr = w_c[:, Q_LORA + KV_LORA:]
    w_c_ext = jnp.concatenate([w_c, jnp.zeros((D, 96), F32), _rot_cols(w_kr, MLA_ROPE // 2),
                               jnp.zeros((D, 96), F32)], axis=1)
    zpad = jnp.zeros((2, 64, RW), F32)
    seg = jnp.arange(RW) // N_A
    w_uq_p = _pad_heads(w_uq, MLA_NOPE + MLA_ROPE, H_C)
    w_uq_rope = w_uq.reshape(Q_LORA, H_C, MLA_NOPE + MLA_ROPE)[:, :, MLA_NOPE:].reshape(Q_LORA, H_C * MLA_ROPE)
    w_uq_rot = _rot_cols(w_uq_rope, MLA_ROPE // 2).reshape(Q_LORA, H_C, MLA_ROPE)
    w_uq_rot_p = jnp.pad(w_uq_rot, ((0, 0), (0, 0), (MLA_NOPE, LANE - MLA_NOPE - MLA_ROPE))).reshape(Q_LORA, H_C * LANE)
    place = jnp.zeros((LANE, H_C, LANE), F32).at[
        jnp.arange(MLA_ROPE)[:, None], jnp.arange(H_C)[None, :], MLA_NOPE + jnp.arange(MLA_ROPE)[:, None]].set(1.0)
    w_uk_p = jnp.concatenate([_pad_heads(w_uk, MLA_NOPE, H_C), place.reshape(LANE, H_C * LANE)], axis=0)
    return dict(
        w_a=w_a.astype(BF16), w_b=w_b.astype(BF16), w_b_rot=_rot_cols(w_b[:, :2 * DIFF_QK], DIFF_HD // 2).astype(BF16),
        w_c=w_c_ext.astype(BF16), w_g=w_g.astype(BF16),
        rwkv=dict(
            mu=mu, w0=w0, a0=a0,
            w2=jnp.stack([jnp.concatenate([w2[0], zpad[0]], 0), jnp.concatenate([zpad[1], w2[1]], 0)]).astype(BF16),
            a2=jnp.stack([jnp.concatenate([a2[0], zpad[0]], 0), jnp.concatenate([zpad[1], a2[1]], 0)]).astype(BF16),
            g2=g2.astype(BF16), k_k=k_k[None], k_a=k_a[None], r_k=r_k.reshape(1, RW),
            ones_bd=(seg[:, None] == seg[None, :]).astype(BF16)),
        ln_w=ln_w[None], ln_b=ln_b[None],
        mla=dict(q_g=q_g[None], kv_g=kv_g[None], w_uq=w_uq_p.astype(BF16), w_uq_rot=w_uq_rot_p.astype(BF16),
                 w_uk=w_uk_p.astype(BF16), w_uv=w_uv.astype(BF16)),
        w_branch=w_branch.astype(BF16), w_out=w_out.astype(BF16),
        wg=wg.astype(BF16), wu=wu.astype(BF16), wd=wd.astype(BF16),
    )


def _mod_rows(m, idx, B):
    lat = m[:B][:, idx]
    ctx = jnp.broadcast_to(m[B][idx][None], lat.shape)
    return jnp.stack([lat, ctx], axis=1)


def kernel(x, c, ctx, c_ctx, ada_w, ada_b, norm1_g, norm2_g, w_in, rwkv_shift_mu, rwkv_w0, rwkv_w2, rwkv_a0,
           rwkv_a2, rwkv_g2, rwkv_k_k, rwkv_k_a, rwkv_r_k, rwkv_ln_w, rwkv_ln_b, diff_lambda, diff_subln_g,
           mla_q_norm_g, mla_kv_norm_g, mla_w_uq, mla_w_uk, mla_w_uv, w_branch, w_out, ffn_w_gate, ffn_w_up,
           ffn_w_down, final_norm_g):
    B, S, _ = x.shape
    L = ada_w.shape[0]
    T = CTX + S
    nt = T // TM
    x_all = jnp.concatenate([ctx, x], axis=1)
    c8 = jnp.concatenate([c, c_ctx[None], jnp.zeros((8 - B - 1, D), F32)], axis=0)
    mod = _modulation(c8, ada_w, ada_b).reshape(L, 8, 6, D)
    tabs = _rope_tables(S)
    ws = (w_in, rwkv_shift_mu, rwkv_w0, rwkv_w2, rwkv_a0, rwkv_a2, rwkv_g2, rwkv_k_k, rwkv_k_a, rwkv_r_k,
          rwkv_ln_w, rwkv_ln_b, mla_q_norm_g, mla_kv_norm_g, mla_w_uq, mla_w_uk, mla_w_uv, w_branch, w_out,
          ffn_w_gate, ffn_w_up, ffn_w_down)
    for l in range(L):
        p = _layer_weights(l, ws)
        mod1 = _mod_rows(mod[l], jnp.array([0, 1]), B)
        g1 = _mod_rows(mod[l], jnp.array([2]), B)
        mod2 = _mod_rows(mod[l], jnp.array([3, 4, 5]), B)
        n1 = norm1_g[l][None]

        za = _inproj(x_all, n1, mod1, p["w_a"])
        zb = _inproj(x_all, n1, mod1, p["w_b"], rope=(p["w_b_rot"], tabs["cos_b"], tabs["sin_b"]))
        zc = _inproj(x_all, n1, mod1, p["w_c"])
        zg = _inproj(x_all, n1, mod1, p["w_g"])

        zeros = lambda n: jnp.zeros((B, n, RWKV_COLS), F32)
        last_rows, first_rows = za[:, TM - 1::TM], za[:, 0::TM]
        hprev = jnp.concatenate([zeros(2), last_rows[:, 1:nt - 1]], axis=1)[:, :, None]
        hnext = jnp.concatenate([zeros(1), first_rows[:, 2:nt], zeros(1)], axis=1)[:, :, None]
        r, v, a, lwf, lwb, kf, kb, bf, bb, g, bonus = _rwkv_prepare(za, hprev, hnext, p["rwkv"])
        yf, yb = _wkv_scan(r, v, a, lwf, lwb, kf, kb, bf, bb)

        yd = _diff_attention(zb, diff_lambda[l], diff_subln_g[l][None], l)

        q, k, vm = _mla_project(zc, dict(p["mla"], cos_q=tabs["cos_q"], sin_q=tabs["sin_q"],
                                         cos_k=tabs["cos_k"], sin_k=tabs["sin_k"]))
        yc = _mla_attention(q, k, vm)

        x_all = _merge(x_all, yf, yb, bonus, g, yd, yc, zg, g1, p["ln_w"], p["ln_b"], p["rwkv"]["ones_bd"],
                       p["w_branch"], p["w_out"])
        x_all = _ffn(x_all, norm2_g[l][None], mod2, p["wg"], p["wu"], p["wd"], final_norm_g[None], l == L - 1)
    return x_all[:, CTX:]
```

```python
import functools
import math

import jax
import jax.numpy as jnp
from jax import lax
from jax.experimental import pallas as pl
from jax.experimental.pallas import tpu as pltpu

F32 = jnp.float32
BF16 = jnp.bfloat16

D = 1024
CTX = 256
GRID_W = 64
ROPE_BASE = 10000.0
NORM_EPS = 1e-6
GN_EPS = 64e-5

H_A, N_A = 8, 64
RW = H_A * N_A
RWKV_COLS = 3 * RW + 2 * 64 + 2 * 64 + 128
H_B, DIFF_HD = 4, 64
DIFF_QK = H_B * 2 * DIFF_HD
DIFF_COLS = 3 * DIFF_QK
H_C = 8
Q_LORA, KV_LORA = 384, 256
MLA_NOPE, MLA_ROPE, MLA_V = 64, 32, 64
MLA_COLS = Q_LORA + KV_LORA + MLA_ROPE
MLA_PAD = 896
D_FF = 2816

TM = 256
LANE = 128
CHUNK = 64
TK = 256
KEY_BLOCK = 1024
LOG2E = 1.4426950408889634
VMEM_LIMIT = 56 * 1024 * 1024


def _cparams(sem):
    return pltpu.CompilerParams(dimension_semantics=sem, vmem_limit_bytes=VMEM_LIMIT)


def _sigmoid(x):
    return 1.0 / (1.0 + jnp.exp(-x))


def _dot(a, b):
    return jnp.dot(a, b, preferred_element_type=F32)


def _dot_nt(a, b):
    return lax.dot_general(a, b, (((1,), (1,)), ((), ())), preferred_element_type=F32)


def _dot_tn(a, b):
    return lax.dot_general(a, b, (((0,), (0,)), ((), ())), preferred_element_type=F32)


def _seg_sum(x, ones_bd):
    hi = x.astype(BF16)
    lo = (x - hi.astype(F32)).astype(BF16)
    return _dot(hi, ones_bd) + _dot(lo, ones_bd)


def _is_ctx(i):
    return jnp.where(i == 0, 1, 0)


def _mod_kernel(c_ref, w_ref, b_ref, o_ref):
    c = c_ref[...]
    s = (c * _sigmoid(c)).astype(BF16)
    o_ref[0] = _dot(s, w_ref[0]) + b_ref[0]


def _modulation(c8, ada_w, ada_b):
    L = ada_w.shape[0]
    tn = 1536
    return pl.pallas_call(
        _mod_kernel,
        grid=(L, 6 * D // tn),
        in_specs=[
            pl.BlockSpec((8, D), lambda l, j: (0, 0)),
            pl.BlockSpec((1, D, tn), lambda l, j: (l, 0, j)),
            pl.BlockSpec((1, 1, tn), lambda l, j: (l, 0, j)),
        ],
        out_specs=pl.BlockSpec((1, 8, tn), lambda l, j: (l, 0, j)),
        out_shape=jax.ShapeDtypeStruct((L, 8, 6 * D), F32),
        compiler_params=_cparams(("parallel", "parallel")),
        name="adaln_mod",
    )(c8, ada_w.astype(BF16), ada_b[:, None, :])


def _norm_mod(x, g, shift, scale):
    ms = jnp.mean(x * x, axis=-1, keepdims=True)
    return (x * lax.rsqrt(ms + NORM_EPS)) * g * (1.0 + scale) + shift


def _inproj_kernel(x_ref, g_ref, mod_ref, w_ref, o_ref):
    h = _norm_mod(x_ref[0], g_ref[...], mod_ref[0, 0, 0:1, :], mod_ref[0, 0, 1:2, :]).astype(BF16)
    o_ref[0] = _dot(h, w_ref[...])


def _inproj_rope_kernel(n_rope, x_ref, g_ref, mod_ref, w_ref, wrot_ref, cos_ref, sin_ref, o_ref):
    h = _norm_mod(x_ref[0], g_ref[...], mod_ref[0, 0, 0:1, :], mod_ref[0, 0, 1:2, :]).astype(BF16)
    z = _dot(h, w_ref[...])
    zr = _dot(h, wrot_ref[...])
    cos = cos_ref[...]
    sin = sin_ref[...]
    for j in range(n_rope // LANE):
        sl = slice(j * LANE, (j + 1) * LANE)
        o_ref[0, :, sl] = z[:, sl] * cos + zr[:, sl] * sin
    o_ref[0, :, n_rope:] = z[:, n_rope:]


def _inproj(x_all, g, mod, w, rope=None):
    B, T, _ = x_all.shape
    N = w.shape[1]
    specs = [
        pl.BlockSpec((1, TM, D), lambda b, i: (b, i, 0)),
        pl.BlockSpec((1, D), lambda b, i: (0, 0)),
        pl.BlockSpec((1, 1, 2, D), lambda b, i: (b, _is_ctx(i), 0, 0)),
        pl.BlockSpec((D, N), lambda b, i: (0, 0)),
    ]
    args = [x_all, g, mod, w]
    if rope is None:
        body = _inproj_kernel
    else:
        wrot, cos, sin = rope
        n_rope = wrot.shape[1]
        body = functools.partial(_inproj_rope_kernel, n_rope)
        specs += [
            pl.BlockSpec((D, n_rope), lambda b, i: (0, 0)),
            pl.BlockSpec((TM, LANE), lambda b, i: (i, 0)),
            pl.BlockSpec((TM, LANE), lambda b, i: (i, 0)),
        ]
        args += [wrot, cos, sin]
    return pl.pallas_call(
        body,
        grid=(B, T // TM),
        in_specs=specs,
        out_specs=pl.BlockSpec((1, TM, N), lambda b, i: (b, i, 0)),
        out_shape=jax.ShapeDtypeStruct((B, T, N), F32),
        compiler_params=_cparams(("parallel", "parallel")),
        name="inproj",
    )(*args)


def _rwkv_prep_kernel(z_ref, hp_ref, hn_ref, mu_ref, w0_ref, w2_ref, a0_ref, a2_ref, g2_ref, kk_ref, ka_ref,
                      rk_ref, ones_ref,
                      r_o, v_o, a_o, lwf_o, lwb_o, kf_o, kb_o, bf_o, bb_o, g_o, bonus_o):
    z = z_ref[0]
    row = lax.broadcasted_iota(jnp.int32, z.shape, 0)
    prev = jnp.where(row == 0, hp_ref[0, 0], pltpu.roll(z, 1, 0))
    nxt = jnp.where(row == TM - 1, hn_ref[0, 0], pltpu.roll(z, TM - 1, 0))
    zs = z + mu_ref[0:1, :] * (prev - z) + mu_ref[1:2, :] * (nxt - z)
    r = zs[:, 0:RW]
    k = zs[:, RW:2 * RW]
    v = zs[:, 2 * RW:3 * RW]
    wd = zs[:, 3 * RW:3 * RW + LANE]
    ad = zs[:, 3 * RW + LANE:3 * RW + 2 * LANE]
    gd = zs[:, 3 * RW + 2 * LANE:3 * RW + 3 * LANE]
    ones_bd = ones_ref[...]
    kk = k * kk_ref[...]
    nrm = jnp.sqrt(_seg_sum(kk * kk, ones_bd))
    kk = kk / jnp.maximum(nrm, 1e-12)
    tw = jnp.tanh(wd).astype(BF16)
    adb = ad.astype(BF16)
    rrk = r * rk_ref[...]
    bonus = jnp.zeros_like(r)
    outs = ((lwf_o, kf_o, bf_o), (lwb_o, kb_o, bb_o))
    for d in range(2):
        u = w0_ref[d:d + 1, :] + _dot(tw, w2_ref[d])
        lw = -math.exp(-0.5) * _sigmoid(u)
        alr = _sigmoid(a0_ref[d:d + 1, :] + _dot(adb, a2_ref[d]))
        kd = k * (1.0 + (alr - 1.0) * ka_ref[...])
        lw_o, k_o, b_o = outs[d]
        lw_o[0] = lw
        k_o[0] = kd
        b_o[0] = kk * alr
        bonus = bonus + _seg_sum(rrk * kd, ones_bd)
    r_o[0] = r
    v_o[0] = v
    a_o[0] = -kk
    g_o[0] = _dot(_sigmoid(gd).astype(BF16), g2_ref[...])
    bonus_o[0] = bonus * v


def _rwkv_prepare(za, hprev, hnext, p):
    B, T, _ = za.shape
    nt = T // TM
    full = lambda a: pl.BlockSpec(a.shape, lambda b, i: (0,) * a.ndim)
    params = [p["mu"], p["w0"], p["w2"], p["a0"], p["a2"], p["g2"], p["k_k"], p["k_a"], p["r_k"], p["ones_bd"]]
    out_spec = pl.BlockSpec((1, TM, RW), lambda b, i: (b, i, 0))
    return pl.pallas_call(
        _rwkv_prep_kernel,
        grid=(B, nt),
        in_specs=[
            pl.BlockSpec((1, TM, RWKV_COLS), lambda b, i: (b, i, 0)),
            pl.BlockSpec((1, 1, 1, RWKV_COLS), lambda b, i: (b, i, 0, 0)),
            pl.BlockSpec((1, 1, 1, RWKV_COLS), lambda b, i: (b, i, 0, 0)),
        ] + [full(a) for a in params],
        out_specs=[out_spec] * 11,
        out_shape=[jax.ShapeDtypeStruct((B, T, RW), F32)] * 11,
        compiler_params=_cparams(("parallel", "parallel")),
        name="rwkv_prepare",
    )(za, hprev, hnext, *params)


def _wkv_consts(rev):
    t = lax.broadcasted_iota(jnp.int32, (CHUNK, LANE), 0)
    lane = lax.broadcasted_iota(jnp.int32, (CHUNK, LANE), 1)
    s = jnp.bitwise_and(lane, CHUNK - 1)
    tt = lax.broadcasted_iota(jnp.int32, (CHUNK, CHUNK), 0)
    ss = lax.broadcasted_iota(jnp.int32, (CHUNK, CHUNK), 1)
    r2 = lax.broadcasted_iota(jnp.int32, (LANE, LANE), 0)
    l2 = lax.broadcasted_iota(jnp.int32, (LANE, LANE), 1)
    return dict(
        strict=(s > t) if rev else (s < t),
        incl=(s >= t) if rev else (s <= t),
        eye=jnp.where(s == t, 1.0, 0.0).astype(F32),
        m0=lane < CHUNK,
        tri=jnp.where((ss >= tt) if rev else (ss <= tt), 1.0, 0.0).astype(F32),
        bdm=(r2 >= CHUNK) == (l2 >= CHUNK),
        ones=jnp.ones((LANE, LANE), F32),
        levels=[((t >> (j + 1)) == (s >> (j + 1))) & ((t >> j) != (s >> j)) for j in range(int(math.log2(CHUNK)))],
    )


def _bd(y, m0):
    z = jnp.zeros_like(y)
    return jnp.concatenate([jnp.where(m0, y, z), jnp.where(m0, z, y)], axis=0)


def _wkv_prepare_chunks(chunks):
    C = CHUNK
    zero = jnp.zeros((C, LANE), F32)
    n = range(len(chunks))
    cs = [ch[6] for ch in chunks]
    m0 = [c["m0"] for c in cs]
    lw = [ch[1] for ch in chunks]
    v = [ch[3] for ch in chunks]
    lw_hi = [x.astype(BF16).astype(F32) for x in lw]
    lw_lo = [lw[i] - lw_hi[i] for i in n]
    cc = [_dot(cs[i]["tri"], jnp.concatenate([lw_hi[i], lw_lo[i]], axis=1)) for i in n]
    tcol = [_dot_tn(jnp.concatenate([lw_hi[i], lw_lo[i]], axis=0), cs[i]["ones"]) for i in n]
    cum = [x[:, :LANE] + x[:, LANE:] for x in cc]
    total = [cum[i][0:1] if chunks[i][7] else cum[i][C - 1:C] for i in n]
    p_inc = [jnp.exp(x) for x in cum]
    p_prev = [jnp.exp(cum[i] - lw[i]) for i in n]
    p_inv = [jnp.exp(-x) for x in cum]
    p_sc = [jnp.exp(total[i] - cum[i]) for i in n]
    rt = [chunks[i][0] * p_inc[i] for i in n]
    at = [chunks[i][4] * p_prev[i] for i in n]
    bt = [chunks[i][5] * p_inv[i] for i in n]
    kt = [chunks[i][2] * p_inv[i] for i in n]
    bks = [jnp.concatenate([chunks[i][5] * p_sc[i], chunks[i][2] * p_sc[i]], axis=0) for i in n]
    pw = [_dot_nt(jnp.concatenate([at[i], rt[i]], axis=0),
                  jnp.concatenate([_bd(bt[i], m0[i]), _bd(kt[i], m0[i])], axis=0)) for i in n]
    ab = [jnp.where(cs[i]["strict"], pw[i][:C, :LANE], zero) for i in n]
    akrk = [jnp.concatenate([jnp.where(cs[i]["strict"], pw[i][:C, LANE:], zero),
                             jnp.where(cs[i]["incl"], pw[i][C:, LANE:], zero)], axis=0) for i in n]
    rb = [jnp.where(cs[i]["incl"], pw[i][C:, :LANE], zero) for i in n]
    tinv = [cs[i]["eye"] + jnp.where(cs[i]["levels"][0], ab[i], zero) for i in n]
    for j in range(1, len(cs[0]["levels"])):
        tl = [_dot(tinv[i], _bd(jnp.where(cs[i]["levels"][j], ab[i], zero), m0[i])) for i in n]
        tinv = [tinv[i] + _dot(tl[i], _bd(tinv[i], m0[i])) for i in n]
    akv = [_dot(akrk[i], _bd(v[i], m0[i])) for i in n]
    wx = [_dot(tinv[i], jnp.concatenate([_bd(at[i], m0[i]), _bd(akv[i][:C], m0[i])], axis=1)) for i in n]
    return [dict(wr=jnp.concatenate([wx[i][:, :LANE], rt[i]], axis=0), x=wx[i][:, LANE:], yk=akv[i][C:], rb=rb[i],
                 bks=bks[i], v=v[i], pcol=jnp.exp(tcol[i]), m0=m0[i], bdm=cs[i]["bdm"]) for i in n]


def _wkv_advance(states, ps):
    C = CHUNK
    n = range(len(ps))
    s1 = [_dot(ps[i]["wr"], states[i]) for i in n]
    u = [s1[i][:C] + ps[i]["x"] for i in n]
    s2 = [_dot(ps[i]["rb"], _bd(u[i], ps[i]["m0"])) for i in n]
    s3 = [_dot_tn(ps[i]["bks"], jnp.concatenate([u[i], ps[i]["v"]], axis=0)) for i in n]
    ys = [s1[i][C:] + s2[i] + ps[i]["yk"] for i in n]
    new = [ps[i]["pcol"] * states[i] + jnp.where(ps[i]["bdm"], s3[i], jnp.zeros_like(s3[i])) for i in n]
    return new, ys


def _wkv_kernel(rf, vf, af, lwf, kf, bf, rb, vb, ab, lwb, kb, bb, yf_o, yb_o, hf_s, hb_s):
    @pl.when(pl.program_id(2) == 0)
    def _():
        hf_s[...] = jnp.zeros_like(hf_s)
        hb_s[...] = jnp.zeros_like(hb_s)

    nch = TM // CHUNK
    dirs = ((False, (rf, lwf, kf, vf, af, bf), yf_o, hf_s), (True, (rb, lwb, kb, vb, ab, bb), yb_o, hb_s))
    order = {False: list(range(nch)), True: list(range(nch - 1, -1, -1))}
    chunks = []
    for rev, refs, _, _ in dirs:
        cs = _wkv_consts(rev)
        for c in order[rev]:
            sl = slice(c * CHUNK, (c + 1) * CHUNK)
            chunks.append(tuple(ref[0, sl, :] for ref in refs) + (cs, rev))
    prepared = _wkv_prepare_chunks(chunks)
    states = [h_s[...] for _, _, _, h_s in dirs]
    for step in range(nch):
        states, ys = _wkv_advance(states, [prepared[d * nch + step] for d in range(len(dirs))])
        for d, (rev, _, y_o, _) in enumerate(dirs):
            c = order[rev][step]
            y_o[0, c * CHUNK:(c + 1) * CHUNK, :] = ys[d]
    for d, (_, _, _, h_s) in enumerate(dirs):
        h_s[...] = states[d]


def _wkv_scan(r, v, a, lwf, lwb, kf, kb, bf, bb):
    B, T, _ = r.shape
    nt = T // TM
    fwd = pl.BlockSpec((1, TM, LANE), lambda b, p, i: (b, i, p))
    bwd = pl.BlockSpec((1, TM, LANE), lambda b, p, i: (b, jnp.where(i == 0, 0, nt - i), p))
    return pl.pallas_call(
        _wkv_kernel,
        grid=(B, RW // LANE, nt),
        in_specs=[fwd] * 6 + [bwd] * 6,
        out_specs=[fwd, bwd],
        out_shape=[jax.ShapeDtypeStruct((B, T, RW), F32)] * 2,
        scratch_shapes=[pltpu.VMEM((LANE, LANE), F32), pltpu.VMEM((LANE, LANE), F32)],
        compiler_params=_cparams(("parallel", "parallel", "arbitrary")),
        name="wkv_scan",
    )(r, v, a, lwf, kf, bf, r, v, a, lwb, kb, bb)


def _load_vt(v_ref, vt_s):
    for c in range(v_ref.shape[1] // TK):
        vt_s[:, c * TK:(c + 1) * TK] = v_ref[0, c * TK:(c + 1) * TK, :].T


def _key_chunks(n_keys, ctx_only):
    chunks = [(0, CTX)]
    if not ctx_only:
        big = min(KEY_BLOCK, n_keys - CTX)
        chunks += [(CTX + j * big, big) for j in range((n_keys - CTX) // big)]
    return chunks


def _attend(streams, k_ref, vt_s, chunks):
    state = [(jnp.full((1, TM), -1e30, F32), jnp.zeros((1, TM), F32),
              jnp.zeros((rows.stop - rows.start, TM), F32)) for _, _, rows in streams]
    for start, size in chunks:
        scores = [_dot_nt(k_ref[0, start:start + size, lanes], q) for q, lanes, _ in streams]
        for j, (_, _, rows) in enumerate(streams):
            m, l, acc = state[j]
            s = scores[j]
            m_new = jnp.maximum(m, jnp.max(s, axis=0, keepdims=True))
            p = jnp.exp2(s - m_new)
            alpha = jnp.exp2(m - m_new)
            state[j] = (m_new, alpha * l + jnp.sum(p, axis=0, keepdims=True),
                        alpha * acc + _dot(vt_s[rows, start:start + size], p))
    return [(acc, l) for _, l, acc in state]


def _diff_attn_kernel(lam_init, q_ref, k_ref, v_ref, lamv_ref, g_ref, o_ref, vt_s):
    i = pl.program_id(2)

    @pl.when(i == 0)
    def _():
        _load_vt(v_ref, vt_s)

    lv = lamv_ref[...]
    lam = (jnp.exp(jnp.sum(lv[0:1] * lv[1:2], axis=-1, keepdims=True))
           - jnp.exp(jnp.sum(lv[2:3] * lv[3:4], axis=-1, keepdims=True)) + lam_init)
    q = q_ref[0] * (DIFF_HD ** -0.5 * LOG2E)
    lane = lax.broadcasted_iota(jnp.int32, q.shape, 1)
    zq = jnp.zeros_like(q)
    q1 = jnp.where(lane < DIFF_HD, q, zq)
    q2 = jnp.where(lane < DIFF_HD, zq, q)
    every = slice(None)
    streams = [(q1, every, slice(0, LANE)), (q2, every, slice(0, LANE))]

    def run(ctx_only):
        (a1, l1), (a2, l2) = _attend(streams, k_ref, vt_s, _key_chunks(k_ref.shape[1], ctx_only))
        o = a1 / l1 - lam * (a2 / l2)
        ms = jnp.mean(o * o, axis=0, keepdims=True)
        o = o * lax.rsqrt(ms + NORM_EPS)
        o_ref[0] = o.T * g_ref[...] * (1.0 - lam_init)

    pl.when(i == 0)(functools.partial(run, True))
    pl.when(i > 0)(functools.partial(run, False))


def _diff_attention(zb, lam_vecs, subln_g, layer_idx):
    B, T, _ = zb.shape
    nt = T // TM
    lam_init = 0.8 - 0.6 * math.exp(-0.3 * layer_idx)
    return pl.pallas_call(
        functools.partial(_diff_attn_kernel, lam_init),
        grid=(B, H_B, nt),
        in_specs=[
            pl.BlockSpec((1, TM, LANE), lambda b, h, i: (b, i, h)),
            pl.BlockSpec((1, T, LANE), lambda b, h, i: (b, 0, H_B + h)),
            pl.BlockSpec((1, T, LANE), lambda b, h, i: (b, 0, 2 * H_B + h)),
            pl.BlockSpec((4, DIFF_HD), lambda b, h, i: (0, 0)),
            pl.BlockSpec((1, LANE), lambda b, h, i: (0, 0)),
        ],
        out_specs=pl.BlockSpec((1, TM, LANE), lambda b, h, i: (b, i, h)),
        out_shape=jax.ShapeDtypeStruct((B, T, H_B * LANE), F32),
        scratch_shapes=[pltpu.VMEM((LANE, T), F32)],
        compiler_params=_cparams(("parallel", "parallel", "arbitrary")),
        name="diff_attn",
    )(zb, zb, zb, lam_vecs, subln_g)


def _mla_proj_kernel(z_ref, qg_ref, kvg_ref, wq_ref, wqr_ref, wk_ref, wv_ref, cq_ref, sq_ref, ck_ref, sk_ref,
                     q_o, k_o, v_o):
    z = z_ref[0]
    cq = z[:, 0:Q_LORA]
    ckv = z[:, Q_LORA:Q_LORA + KV_LORA]
    kr = z[:, 5 * LANE:6 * LANE] * ck_ref[...] + z[:, 6 * LANE:7 * LANE] * sk_ref[...]
    rms = lambda t: t * lax.rsqrt(jnp.mean(t * t, axis=-1, keepdims=True) + NORM_EPS)
    cqn = (rms(cq) * qg_ref[...]).astype(BF16)
    ckvn = rms(ckv) * kvg_ref[...]
    q = _dot(cqn, wq_ref[...])
    qr = _dot(cqn, wqr_ref[...])
    cos = cq_ref[...]
    sin = sq_ref[...]
    scale = (MLA_NOPE + MLA_ROPE) ** -0.5 * LOG2E
    for h in range(H_C):
        sl = slice(h * LANE, (h + 1) * LANE)
        q_o[0, :, sl] = (q[:, sl] * cos + qr[:, sl] * sin) * scale
    kin = jnp.concatenate([ckvn, kr], axis=1).astype(BF16)
    k_o[0] = _dot(kin, wk_ref[...])
    v_o[0] = _dot(ckvn.astype(BF16), wv_ref[...])


def _mla_project(zc, p):
    B, T, _ = zc.shape
    full = lambda a: pl.BlockSpec(a.shape, lambda b, i: (0,) * a.ndim)
    tab = pl.BlockSpec((TM, LANE), lambda b, i: (i, 0))
    ws = [p["q_g"], p["kv_g"], p["w_uq"], p["w_uq_rot"], p["w_uk"], p["w_uv"]]
    return pl.pallas_call(
        _mla_proj_kernel,
        grid=(B, T // TM),
        in_specs=[pl.BlockSpec((1, TM, MLA_PAD), lambda b, i: (b, i, 0))] + [full(a) for a in ws] + [tab] * 4,
        out_specs=[
            pl.BlockSpec((1, TM, H_C * LANE), lambda b, i: (b, i, 0)),
            pl.BlockSpec((1, TM, H_C * LANE), lambda b, i: (b, i, 0)),
            pl.BlockSpec((1, TM, H_C * MLA_V), lambda b, i: (b, i, 0)),
        ],
        out_shape=[
            jax.ShapeDtypeStruct((B, T, H_C * LANE), F32),
            jax.ShapeDtypeStruct((B, T, H_C * LANE), F32),
            jax.ShapeDtypeStruct((B, T, H_C * MLA_V), F32),
        ],
        compiler_params=_cparams(("parallel", "parallel")),
        name="mla_project",
    )(zc, *ws, p["cos_q"], p["sin_q"], p["cos_k"], p["sin_k"])


def _mla_attn_kernel(q_ref, k_ref, v_ref, o_ref, vt_s):
    i = pl.program_id(2)

    @pl.when(i == 0)
    def _():
        _load_vt(v_ref, vt_s)

    q = q_ref[0]
    streams = [(q[:, :LANE], slice(0, LANE), slice(0, MLA_V)),
               (q[:, LANE:], slice(LANE, 2 * LANE), slice(MLA_V, 2 * MLA_V))]

    def run(ctx_only):
        (a0, l0), (a1, l1) = _attend(streams, k_ref, vt_s, _key_chunks(k_ref.shape[1], ctx_only))
        o_ref[0] = jnp.concatenate([a0 / l0, a1 / l1], axis=0).T

    pl.when(i == 0)(functools.partial(run, True))
    pl.when(i > 0)(functools.partial(run, False))


def _mla_attention(q, k, v):
    B, T, _ = q.shape
    nt = T // TM
    return pl.pallas_call(
        _mla_attn_kernel,
        grid=(B, H_C // 2, nt),
        in_specs=[
            pl.BlockSpec((1, TM, 2 * LANE), lambda b, p, i: (b, i, p)),
            pl.BlockSpec((1, T, 2 * LANE), lambda b, p, i: (b, 0, p)),
            pl.BlockSpec((1, T, LANE), lambda b, p, i: (b, 0, p)),
        ],
        out_specs=pl.BlockSpec((1, TM, LANE), lambda b, p, i: (b, i, p)),
        out_shape=jax.ShapeDtypeStruct((B, T, H_C * MLA_V), F32),
        scratch_shapes=[pltpu.VMEM((LANE, T), F32)],
        compiler_params=_cparams(("parallel", "parallel", "arbitrary")),
        name="mla_attn",
    )(q, k, v)


def _merge_kernel(x_ref, yf_ref, yb_ref, bonus_ref, g_ref, yd_ref, yc_ref, zg_ref, g1_ref, lnw_ref, lnb_ref,
                  ones_ref, wb_ref, wo_ref, o_ref):
    ones_bd = ones_ref[...]
    y = yf_ref[0] + yb_ref[0]
    mean = _seg_sum(y, ones_bd) * (1.0 / N_A)
    yc = y - mean
    var = _seg_sum(yc * yc, ones_bd) * (1.0 / N_A)
    ya = (yc * lax.rsqrt(var + GN_EPS) * lnw_ref[...] + lnb_ref[...] + bonus_ref[0]) * g_ref[0]
    acc = None
    for j, yj in enumerate((ya, yd_ref[0], yc_ref[0])):
        gate = _sigmoid(zg_ref[0, :, j * D:(j + 1) * D])
        t = gate * _dot(yj.astype(BF16), wb_ref[j])
        acc = t if acc is None else acc + t
    o_ref[0] = x_ref[0] + g1_ref[0, 0] * _dot(acc.astype(BF16), wo_ref[...])


def _merge(x_all, yf, yb, bonus, g, yd, yc, zg, g1, lnw, lnb, ones_bd, wb, wo):
    B, T, _ = x_all.shape
    row = lambda n: pl.BlockSpec((1, TM, n), lambda b, i: (b, i, 0))
    full = lambda a: pl.BlockSpec(a.shape, lambda b, i: (0,) * a.ndim)
    return pl.pallas_call(
        _merge_kernel,
        grid=(B, T // TM),
        in_specs=[row(D)] + [row(RW)] * 6 + [row(3 * D),
                  pl.BlockSpec((1, 1, 1, D), lambda b, i: (b, _is_ctx(i), 0, 0)),
                  full(lnw), full(lnb), full(ones_bd), full(wb), full(wo)],
        out_specs=row(D),
        out_shape=jax.ShapeDtypeStruct((B, T, D), F32),
        compiler_params=_cparams(("parallel", "parallel")),
        name="merge",
    )(x_all, yf, yb, bonus, g, yd, yc, zg, g1, lnw, lnb, ones_bd, wb, wo)


def _ffn_kernel(final, x_ref, g_ref, mod_ref, wg_ref, wu_ref, wd_ref, fg_ref, o_ref):
    x = x_ref[0]
    h = _norm_mod(x, g_ref[...], mod_ref[0, 0, 0:1, :], mod_ref[0, 0, 1:2, :]).astype(BF16)
    gate = _dot(h, wg_ref[...])
    up = _dot(h, wu_ref[...])
    act = (gate * _sigmoid(gate) * up).astype(BF16)
    out = x + mod_ref[0, 0, 2:3, :] * _dot(act, wd_ref[...])
    if final:
        ms = jnp.mean(out * out, axis=-1, keepdims=True)
        out = out * lax.rsqrt(ms + NORM_EPS) * fg_ref[...]
    o_ref[0] = out


def _ffn(x_all, g, mod, wg, wu, wd, fg, final):
    B, T, _ = x_all.shape
    full = lambda a: pl.BlockSpec(a.shape, lambda b, i: (0,) * a.ndim, pipeline_mode=pl.Buffered(1))
    return pl.pallas_call(
        functools.partial(_ffn_kernel, final),
        grid=(B, T // TM),
        in_specs=[
            pl.BlockSpec((1, TM, D), lambda b, i: (b, i, 0)),
            pl.BlockSpec((1, D), lambda b, i: (0, 0)),
            pl.BlockSpec((1, 1, 3, D), lambda b, i: (b, _is_ctx(i), 0, 0)),
            full(wg), full(wu), full(wd),
            pl.BlockSpec((1, D), lambda b, i: (0, 0)),
        ],
        out_specs=pl.BlockSpec((1, TM, D), lambda b, i: (b, i, 0)),
        out_shape=jax.ShapeDtypeStruct((B, T, D), F32),
        compiler_params=_cparams(("parallel", "parallel")),
        name="ffn",
    )(x_all, g, mod, wg, wu, wd, fg)


def _rot_cols(w, half):
    d, n = w.shape
    w4 = w.reshape(d, n // (2 * half), 2, half)
    return jnp.concatenate([-w4[:, :, 1:2], w4[:, :, 0:1]], axis=2).reshape(d, n)


def _rope_tables(n_lat):
    rows = n_lat // GRID_W
    row = jnp.repeat(jnp.arange(rows, dtype=F32), GRID_W)
    col = jnp.tile(jnp.arange(GRID_W, dtype=F32), rows)

    def tables(rot_dim):
        n_freq = rot_dim // 4
        inv_freq = ROPE_BASE ** (-jnp.arange(n_freq, dtype=F32) / n_freq)
        ang = jnp.concatenate([row[:, None] * inv_freq, col[:, None] * inv_freq], axis=-1)
        cos = jnp.concatenate([jnp.ones((CTX, rot_dim // 2), F32), jnp.cos(ang)], axis=0)
        sin = jnp.concatenate([jnp.zeros((CTX, rot_dim // 2), F32), jnp.sin(ang)], axis=0)
        return cos, sin

    T = CTX + n_lat
    cos_b, sin_b = tables(DIFF_HD)
    cos_c, sin_c = tables(MLA_ROPE)
    one = lambda n: jnp.ones((T, n), F32)
    zero = lambda n: jnp.zeros((T, n), F32)
    return dict(
        cos_b=jnp.tile(cos_b, (1, 4)), sin_b=jnp.tile(sin_b, (1, 4)),
        cos_q=jnp.concatenate([one(MLA_NOPE), cos_c, cos_c, one(32)], axis=1),
        sin_q=jnp.concatenate([zero(MLA_NOPE), sin_c, sin_c, zero(32)], axis=1),
        cos_k=jnp.concatenate([cos_c, cos_c, one(96)], axis=1),
        sin_k=jnp.concatenate([sin_c, sin_c, zero(96)], axis=1),
    )


def _pad_heads(w, per_head, n_heads):
    d = w.shape[0]
    w3 = w.reshape(d, n_heads, per_head)
    return jnp.pad(w3, ((0, 0), (0, 0), (0, LANE - per_head))).reshape(d, n_heads * LANE)


def _layer_weights(l, ws):
    (w_in, mu, w0, w2, a0, a2, g2, k_k, k_a, r_k, ln_w, ln_b, q_g, kv_g, w_uq, w_uk, w_uv, w_branch, w_out,
     wg, wu, wd) = [w[l] for w in ws]
    o1, o2, o3 = RWKV_COLS, RWKV_COLS + DIFF_COLS, RWKV_COLS + DIFF_COLS + MLA_COLS
    w_a, w_b, w_c, w_g = w_in[:, :o1], w_in[:, o1:o2], w_in[:, o2:o3], w_in[:, o3:]
    w_kr = w_c[:, Q_LORA + KV_LORA:]
    w_c_ext = jnp.concatenate([w_c, jnp.zeros((D, 96), F32), _rot_cols(w_kr, MLA_ROPE // 2),
                               jnp.zeros((D, 96), F32)], axis=1)
    zpad = jnp.zeros((2, 64, RW), F32)
    seg = jnp.arange(RW) // N_A
    w_uq_p = _pad_heads(w_uq, MLA_NOPE + MLA_ROPE, H_C)
    w_uq_rope = w_uq.reshape(Q_LORA, H_C, MLA_NOPE + MLA_ROPE)[:, :, MLA_NOPE:].reshape(Q_LORA, H_C * MLA_ROPE)
    w_uq_rot = _rot_cols(w_uq_rope, MLA_ROPE // 2).reshape(Q_LORA, H_C, MLA_ROPE)
    w_uq_rot_p = jnp.pad(w_uq_rot, ((0, 0), (0, 0), (MLA_NOPE, LANE - MLA_NOPE - MLA_ROPE))).reshape(Q_LORA, H_C * LANE)
    place = jnp.zeros((LANE, H_C, LANE), F32).at[
        jnp.arange(MLA_ROPE)[:, None], jnp.arange(H_C)[None, :], MLA_NOPE + jnp.arange(MLA_ROPE)[:, None]].set(1.0)
    w_uk_p = jnp.concatenate([_pad_heads(w_uk, MLA_NOPE, H_C), place.reshape(LANE, H_C * LANE)], axis=0)
    return dict(
        w_a=w_a.astype(BF16), w_b=w_b.astype(BF16), w_b_rot=_rot_cols(w_b[:, :2 * DIFF_QK], DIFF_HD // 2).astype(BF16),
        w_c=w_c_ext.astype(BF16), w_g=w_g.astype(BF16),
        rwkv=dict(
            mu=mu, w0=w0, a0=a0,
            w2=jnp.stack([jnp.concatenate([w2[0], zpad[0]], 0), jnp.concatenate([zpad[1], w2[1]], 0)]).astype(BF16),
            a2=jnp.stack([jnp.concatenate([a2[0], zpad[0]], 0), jnp.concatenate([zpad[1], a2[1]], 0)]).astype(BF16),
            g2=g2.astype(BF16), k_k=k_k[None], k_a=k_a[None], r_k=r_k.reshape(1, RW),
            ones_bd=(seg[:, None] == seg[None, :]).astype(BF16)),
        ln_w=ln_w[None], ln_b=ln_b[None],
        mla=dict(q_g=q_g[None], kv_g=kv_g[None], w_uq=w_uq_p.astype(BF16), w_uq_rot=w_uq_rot_p.astype(BF16),
                 w_uk=w_uk_p.astype(BF16), w_uv=w_uv.astype(BF16)),
        w_branch=w_branch.astype(BF16), w_out=w_out.astype(BF16),
        wg=wg.astype(BF16), wu=wu.astype(BF16), wd=wd.astype(BF16),
    )


def _mod_rows(m, idx, B):
    lat = m[:B][:, idx]
    ctx = jnp.broadcast_to(m[B][idx][None], lat.shape)
    return jnp.stack([lat, ctx], axis=1)


def kernel(x, c, ctx, c_ctx, ada_w, ada_b, norm1_g, norm2_g, w_in, rwkv_shift_mu, rwkv_w0, rwkv_w2, rwkv_a0,
           rwkv_a2, rwkv_g2, rwkv_k_k, rwkv_k_a, rwkv_r_k, rwkv_ln_w, rwkv_ln_b, diff_lambda, diff_subln_g,
           mla_q_norm_g, mla_kv_norm_g, mla_w_uq, mla_w_uk, mla_w_uv, w_branch, w_out, ffn_w_gate, ffn_w_up,
           ffn_w_down, final_norm_g):
    B, S, _ = x.shape
    L = ada_w.shape[0]
    T = CTX + S
    nt = T // TM
    x_all = jnp.concatenate([ctx, x], axis=1)
    c8 = jnp.concatenate([c, c_ctx[None], jnp.zeros((8 - B - 1, D), F32)], axis=0)
    mod = _modulation(c8, ada_w, ada_b).reshape(L, 8, 6, D)
    tabs = _rope_tables(S)
    ws = (w_in, rwkv_shift_mu, rwkv_w0, rwkv_w2, rwkv_a0, rwkv_a2, rwkv_g2, rwkv_k_k, rwkv_k_a, rwkv_r_k,
          rwkv_ln_w, rwkv_ln_b, mla_q_norm_g, mla_kv_norm_g, mla_w_uq, mla_w_uk, mla_w_uv, w_branch, w_out,
          ffn_w_gate, ffn_w_up, ffn_w_down)
    for l in range(L):
        p = _layer_weights(l, ws)
        mod1 = _mod_rows(mod[l], jnp.array([0, 1]), B)
        g1 = _mod_rows(mod[l], jnp.array([2]), B)
        mod2 = _mod_rows(mod[l], jnp.array([3, 4, 5]), B)
        n1 = norm1_g[l][None]

        za = _inproj(x_all, n1, mod1, p["w_a"])
        zb = _inproj(x_all, n1, mod1, p["w_b"], rope=(p["w_b_rot"], tabs["cos_b"], tabs["sin_b"]))
        zc = _inproj(x_all, n1, mod1, p["w_c"])
        zg = _inproj(x_all, n1, mod1, p["w_g"])

        zeros = lambda n: jnp.zeros((B, n, RWKV_COLS), F32)
        last_rows, first_rows = za[:, TM - 1::TM], za[:, 0::TM]
        hprev = jnp.concatenate([zeros(2), last_rows[:, 1:nt - 1]], axis=1)[:, :, None]
        hnext = jnp.concatenate([zeros(1), first_rows[:, 2:nt], zeros(1)], axis=1)[:, :, None]
        r, v, a, lwf, lwb, kf, kb, bf, bb, g, bonus = _rwkv_prepare(za, hprev, hnext, p["rwkv"])
        yf, yb = _wkv_scan(r, v, a, lwf, lwb, kf, kb, bf, bb)

        yd = _diff_attention(zb, diff_lambda[l], diff_subln_g[l][None], l)

        q, k, vm = _mla_project(zc, dict(p["mla"], cos_q=tabs["cos_q"], sin_q=tabs["sin_q"],
                                         cos_k=tabs["cos_k"], sin_k=tabs["sin_k"]))
        yc = _mla_attention(q, k, vm)

        x_all = _merge(x_all, yf, yb, bonus, g, yd, yc, zg, g1, p["ln_w"], p["ln_b"], p["rwkv"]["ones_bd"],
                       p["w_branch"], p["w_out"])
        x_all = _ffn(x_all, norm2_g[l][None], mod2, p["wg"], p["wu"], p["wd"], final_norm_g[None], l == L - 1)
    return x_all[:, CTX:]
```

```python
import functools
import math

import jax
import jax.numpy as jnp
from jax import lax
from jax.experimental import pallas as pl
from jax.experimental.pallas import tpu as pltpu

F32 = jnp.float32
BF16 = jnp.bfloat16

D = 1024
CTX = 256
GRID_W = 64
ROPE_BASE = 10000.0
NORM_EPS = 1e-6
GN_EPS = 64e-5

H_A, N_A = 8, 64
RW = H_A * N_A
RWKV_COLS = 3 * RW + 2 * 64 + 2 * 64 + 128
H_B, DIFF_HD = 4, 64
DIFF_QK = H_B * 2 * DIFF_HD
DIFF_COLS = 3 * DIFF_QK
H_C = 8
Q_LORA, KV_LORA = 384, 256
MLA_NOPE, MLA_ROPE, MLA_V = 64, 32, 64
MLA_COLS = Q_LORA + KV_LORA + MLA_ROPE
MLA_PAD = 896
D_FF = 2816

TM = 256
LANE = 128
CHUNK = 64
TK = 256
KEY_BLOCK = 2048
LOG2E = 1.4426950408889634
VMEM_LIMIT = 56 * 1024 * 1024


def _cparams(sem):
    return pltpu.CompilerParams(dimension_semantics=sem, vmem_limit_bytes=VMEM_LIMIT)


def _sigmoid(x):
    return 1.0 / (1.0 + jnp.exp(-x))


def _dot(a, b):
    return jnp.dot(a, b, preferred_element_type=F32)


def _dot_nt(a, b):
    return lax.dot_general(a, b, (((1,), (1,)), ((), ())), preferred_element_type=F32)


def _dot_tn(a, b):
    return lax.dot_general(a, b, (((0,), (0,)), ((), ())), preferred_element_type=F32)


def _seg_sum(x, ones_bd):
    hi = x.astype(BF16)
    lo = (x - hi.astype(F32)).astype(BF16)
    return _dot(hi, ones_bd) + _dot(lo, ones_bd)


def _is_ctx(i):
    return jnp.where(i == 0, 1, 0)


def _mod_kernel(c_ref, w_ref, b_ref, o_ref):
    c = c_ref[...]
    s = (c * _sigmoid(c)).astype(BF16)
    o_ref[0] = _dot(s, w_ref[0]) + b_ref[0]


def _modulation(c8, ada_w, ada_b):
    L = ada_w.shape[0]
    tn = 1536
    return pl.pallas_call(
        _mod_kernel,
        grid=(L, 6 * D // tn),
        in_specs=[
            pl.BlockSpec((8, D), lambda l, j: (0, 0)),
            pl.BlockSpec((1, D, tn), lambda l, j: (l, 0, j)),
            pl.BlockSpec((1, 1, tn), lambda l, j: (l, 0, j)),
        ],
        out_specs=pl.BlockSpec((1, 8, tn), lambda l, j: (l, 0, j)),
        out_shape=jax.ShapeDtypeStruct((L, 8, 6 * D), F32),
        compiler_params=_cparams(("parallel", "parallel")),
        name="adaln_mod",
    )(c8, ada_w.astype(BF16), ada_b[:, None, :])


def _norm_mod(x, g, shift, scale):
    ms = jnp.mean(x * x, axis=-1, keepdims=True)
    return (x * lax.rsqrt(ms + NORM_EPS)) * g * (1.0 + scale) + shift


N_ROPE = 2 * DIFF_QK
IN_SPLITS = (RWKV_COLS, MLA_PAD, 3 * D, DIFF_COLS, N_ROPE)
HALO = 8


def _inproj_kernel(x_ref, g_ref, mod_ref, w_ref, cos_ref, sin_ref, za_o, zc_o, zg_o, zb_o, first_o, last_o):
    h = _norm_mod(x_ref[0], g_ref[...], mod_ref[0, 0, 0:1, :], mod_ref[0, 0, 1:2, :]).astype(BF16)
    ofs = [0]
    for n in IN_SPLITS:
        ofs.append(ofs[-1] + n)
    proj = lambda j: _dot(h, w_ref[:, ofs[j]:ofs[j + 1]])
    za = proj(0)
    za_o[0] = za
    first_o[0, 0] = za[0:HALO]
    last_o[0, 0] = za[TM - HALO:TM]
    zc_o[0] = proj(1)
    zg_o[0] = proj(2)
    z = proj(3)
    zr = proj(4)
    cos = cos_ref[...]
    sin = sin_ref[...]
    for j in range(N_ROPE // LANE):
        sl = slice(j * LANE, (j + 1) * LANE)
        zb_o[0, :, sl] = z[:, sl] * cos + zr[:, sl] * sin
    zb_o[0, :, N_ROPE:] = z[:, N_ROPE:]


def _inproj(x_all, g, mod, w, cos, sin):
    B, T, _ = x_all.shape
    nt = T // TM
    row = lambda n: pl.BlockSpec((1, TM, n), lambda b, i: (b, i, 0))
    edge = pl.BlockSpec((1, 1, HALO, RWKV_COLS), lambda b, i: (b, i, 0, 0))
    widths = (RWKV_COLS, MLA_PAD, 3 * D, DIFF_COLS)
    return pl.pallas_call(
        _inproj_kernel,
        grid=(B, nt),
        in_specs=[
            row(D),
            pl.BlockSpec((1, D), lambda b, i: (0, 0)),
            pl.BlockSpec((1, 1, 2, D), lambda b, i: (b, _is_ctx(i), 0, 0)),
            pl.BlockSpec(w.shape, lambda b, i: (0, 0), pipeline_mode=pl.Buffered(1)),
            pl.BlockSpec((TM, LANE), lambda b, i: (i, 0)),
            pl.BlockSpec((TM, LANE), lambda b, i: (i, 0)),
        ],
        out_specs=[row(n) for n in widths] + [edge, edge],
        out_shape=[jax.ShapeDtypeStruct((B, T, n), F32) for n in widths]
        + [jax.ShapeDtypeStruct((B, nt, HALO, RWKV_COLS), F32)] * 2,
        compiler_params=_cparams(("parallel", "parallel")),
        name="inproj",
    )(x_all, g, mod, w, cos, sin)


def _rwkv_prep_kernel(z_ref, hp_ref, hn_ref, mu_ref, w0_ref, w2_ref, a0_ref, a2_ref, g2_ref, kk_ref, ka_ref,
                      rk_ref, ones_ref,
                      r_o, v_o, a_o, lwf_o, lwb_o, kf_o, kb_o, bf_o, bb_o, g_o, bonus_o):
    z = z_ref[0]
    row = lax.broadcasted_iota(jnp.int32, z.shape, 0)
    prev = jnp.where(row == 0, hp_ref[0, 0], pltpu.roll(z, 1, 0))
    nxt = jnp.where(row == TM - 1, hn_ref[0, 0], pltpu.roll(z, TM - 1, 0))
    zs = z + mu_ref[0:1, :] * (prev - z) + mu_ref[1:2, :] * (nxt - z)
    r = zs[:, 0:RW]
    k = zs[:, RW:2 * RW]
    v = zs[:, 2 * RW:3 * RW]
    wd = zs[:, 3 * RW:3 * RW + LANE]
    ad = zs[:, 3 * RW + LANE:3 * RW + 2 * LANE]
    gd = zs[:, 3 * RW + 2 * LANE:3 * RW + 3 * LANE]
    ones_bd = ones_ref[...]
    kk = k * kk_ref[...]
    nrm = jnp.sqrt(_seg_sum(kk * kk, ones_bd))
    kk = kk / jnp.maximum(nrm, 1e-12)
    tw = jnp.tanh(wd).astype(BF16)
    adb = ad.astype(BF16)
    rrk = r * rk_ref[...]
    bonus = jnp.zeros_like(r)
    outs = ((lwf_o, kf_o, bf_o), (lwb_o, kb_o, bb_o))
    for d in range(2):
        u = w0_ref[d:d + 1, :] + _dot(tw, w2_ref[d])
        lw = -math.exp(-0.5) * _sigmoid(u)
        alr = _sigmoid(a0_ref[d:d + 1, :] + _dot(adb, a2_ref[d]))
        kd = k * (1.0 + (alr - 1.0) * ka_ref[...])
        lw_o, k_o, b_o = outs[d]
        lw_o[0] = lw
        k_o[0] = kd
        b_o[0] = kk * alr
        bonus = bonus + _seg_sum(rrk * kd, ones_bd)
    r_o[0] = r
    v_o[0] = v
    a_o[0] = -kk
    g_o[0] = _dot(_sigmoid(gd).astype(BF16), g2_ref[...])
    bonus_o[0] = bonus * v


def _rwkv_prepare(za, hprev, hnext, p):
    B, T, _ = za.shape
    nt = T // TM
    full = lambda a: pl.BlockSpec(a.shape, lambda b, i: (0,) * a.ndim)
    params = [p["mu"], p["w0"], p["w2"], p["a0"], p["a2"], p["g2"], p["k_k"], p["k_a"], p["r_k"], p["ones_bd"]]
    out_spec = pl.BlockSpec((1, TM, RW), lambda b, i: (b, i, 0))
    return pl.pallas_call(
        _rwkv_prep_kernel,
        grid=(B, nt),
        in_specs=[
            pl.BlockSpec((1, TM, RWKV_COLS), lambda b, i: (b, i, 0)),
            pl.BlockSpec((1, 1, 1, RWKV_COLS), lambda b, i: (b, i, 0, 0)),
            pl.BlockSpec((1, 1, 1, RWKV_COLS), lambda b, i: (b, i, 0, 0)),
        ] + [full(a) for a in params],
        out_specs=[out_spec] * 11,
        out_shape=[jax.ShapeDtypeStruct((B, T, RW), F32)] * 11,
        compiler_params=_cparams(("parallel", "parallel")),
        name="rwkv_prepare",
    )(za, hprev, hnext, *params)


def _wkv_consts(rev):
    t = lax.broadcasted_iota(jnp.int32, (CHUNK, LANE), 0)
    lane = lax.broadcasted_iota(jnp.int32, (CHUNK, LANE), 1)
    s = jnp.bitwise_and(lane, CHUNK - 1)
    tt = lax.broadcasted_iota(jnp.int32, (CHUNK, CHUNK), 0)
    ss = lax.broadcasted_iota(jnp.int32, (CHUNK, CHUNK), 1)
    r2 = lax.broadcasted_iota(jnp.int32, (LANE, LANE), 0)
    l2 = lax.broadcasted_iota(jnp.int32, (LANE, LANE), 1)
    return dict(
        strict=(s > t) if rev else (s < t),
        incl=(s >= t) if rev else (s <= t),
        eye=jnp.where(s == t, 1.0, 0.0).astype(F32),
        m0=lane < CHUNK,
        tri=jnp.where((ss >= tt) if rev else (ss <= tt), 1.0, 0.0).astype(F32),
        bdm=(r2 >= CHUNK) == (l2 >= CHUNK),
        ones=jnp.ones((LANE, LANE), F32),
        levels=[((t >> (j + 1)) == (s >> (j + 1))) & ((t >> j) != (s >> j)) for j in range(int(math.log2(CHUNK)))],
    )


def _bd(y, m0):
    z = jnp.zeros_like(y)
    return jnp.concatenate([jnp.where(m0, y, z), jnp.where(m0, z, y)], axis=0)


def _wkv_prepare_chunks(chunks):
    C = CHUNK
    zero = jnp.zeros((C, LANE), F32)
    n = range(len(chunks))
    cs = [ch[6] for ch in chunks]
    m0 = [c["m0"] for c in cs]
    lw = [ch[1] for ch in chunks]
    v = [ch[3] for ch in chunks]
    lw_hi = [x.astype(BF16).astype(F32) for x in lw]
    lw_lo = [lw[i] - lw_hi[i] for i in n]
    cc = [_dot(cs[i]["tri"], jnp.concatenate([lw_hi[i], lw_lo[i]], axis=1)) for i in n]
    tcol = [_dot_tn(jnp.concatenate([lw_hi[i], lw_lo[i]], axis=0), cs[i]["ones"]) for i in n]
    cum = [x[:, :LANE] + x[:, LANE:] for x in cc]
    total = [cum[i][0:1] if chunks[i][7] else cum[i][C - 1:C] for i in n]
    p_inc = [jnp.exp(x) for x in cum]
    p_prev = [jnp.exp(cum[i] - lw[i]) for i in n]
    p_inv = [jnp.exp(-x) for x in cum]
    p_sc = [jnp.exp(total[i] - cum[i]) for i in n]
    rt = [chunks[i][0] * p_inc[i] for i in n]
    at = [chunks[i][4] * p_prev[i] for i in n]
    bt = [chunks[i][5] * p_inv[i] for i in n]
    kt = [chunks[i][2] * p_inv[i] for i in n]
    bks = [jnp.concatenate([chunks[i][5] * p_sc[i], chunks[i][2] * p_sc[i]], axis=0) for i in n]
    pw = [_dot_nt(jnp.concatenate([at[i], rt[i]], axis=0),
                  jnp.concatenate([_bd(bt[i], m0[i]), _bd(kt[i], m0[i])], axis=0)) for i in n]
    ab = [jnp.where(cs[i]["strict"], pw[i][:C, :LANE], zero) for i in n]
    akrk = [jnp.concatenate([jnp.where(cs[i]["strict"], pw[i][:C, LANE:], zero),
                             jnp.where(cs[i]["incl"], pw[i][C:, LANE:], zero)], axis=0) for i in n]
    rb = [jnp.where(cs[i]["incl"], pw[i][C:, :LANE], zero) for i in n]
    tinv = [cs[i]["eye"] + jnp.where(cs[i]["levels"][0], ab[i], zero) for i in n]
    for j in range(1, len(cs[0]["levels"])):
        tl = [_dot(tinv[i], _bd(jnp.where(cs[i]["levels"][j], ab[i], zero), m0[i])) for i in n]
        tinv = [tinv[i] + _dot(tl[i], _bd(tinv[i], m0[i])) for i in n]
    akv = [_dot(akrk[i], _bd(v[i], m0[i])) for i in n]
    wx = [_dot(tinv[i], jnp.concatenate([_bd(at[i], m0[i]), _bd(akv[i][:C], m0[i])], axis=1)) for i in n]
    return [dict(wr=jnp.concatenate([wx[i][:, :LANE], rt[i]], axis=0), x=wx[i][:, LANE:], yk=akv[i][C:], rb=rb[i],
                 bks=bks[i], v=v[i], pcol=jnp.exp(tcol[i])) for i in n]


WKV_PAIRS = 4


def _wkv_kernel(rf, vf, af, lwf, kf, bf, rb, vb, ab, lwb, kb, bb, yf_o, yb_o, h_s):
    @pl.when(pl.program_id(2) == 0)
    def _():
        h_s[...] = jnp.zeros_like(h_s)

    C = CHUNK
    nch = TM // C
    dirs = ((False, (rf, lwf, kf, vf, af, bf), yf_o), (True, (rb, lwb, kb, vb, ab, bb), yb_o))
    order = {False: list(range(nch)), True: list(range(nch - 1, -1, -1))}
    chains = [(rev, refs, y_o, slice(p * LANE, (p + 1) * LANE)) for rev, refs, y_o in dirs for p in range(WKV_PAIRS)]
    consts = {rev: _wkv_consts(rev) for rev, _, _ in dirs}
    chunks = []
    for rev, refs, _, lanes in chains:
        for c in order[rev]:
            chunks.append(tuple(ref[0, c * C:(c + 1) * C, lanes] for ref in refs) + (consts[rev], rev))
    prepared = _wkv_prepare_chunks(chunks)
    m0 = consts[False]["m0"]
    bdm = consts[False]["bdm"]
    nc = range(len(chains))
    states = [h_s[j] for j in nc]
    for step in range(nch):
        ps = [prepared[j * nch + step] for j in nc]
        s1 = [_dot(ps[j]["wr"], states[j]) for j in nc]
        u = [s1[j][:C] + ps[j]["x"] for j in nc]
        s2 = [_dot(ps[j]["rb"], _bd(u[j], m0)) for j in nc]
        s3 = [_dot_tn(ps[j]["bks"], jnp.concatenate([u[j], ps[j]["v"]], axis=0)) for j in nc]
        for j, (rev, _, y_o, lanes) in enumerate(chains):
            c = order[rev][step]
            y_o[0, c * C:(c + 1) * C, lanes] = s1[j][C:] + s2[j] + ps[j]["yk"]
        states = [ps[j]["pcol"] * states[j] + jnp.where(bdm, s3[j], jnp.zeros_like(s3[j])) for j in nc]
    for j in nc:
        h_s[j] = states[j]


def _wkv_scan(r, v, a, lwf, lwb, kf, kb, bf, bb):
    B, T, _ = r.shape
    nt = T // TM
    width = WKV_PAIRS * LANE
    fwd = pl.BlockSpec((1, TM, width), lambda b, p, i: (b, i, p))
    bwd = pl.BlockSpec((1, TM, width), lambda b, p, i: (b, jnp.where(i == 0, 0, nt - i), p))
    return pl.pallas_call(
        _wkv_kernel,
        grid=(B, RW // width, nt),
        in_specs=[fwd] * 6 + [bwd] * 6,
        out_specs=[fwd, bwd],
        out_shape=[jax.ShapeDtypeStruct((B, T, RW), F32)] * 2,
        scratch_shapes=[pltpu.VMEM((2 * WKV_PAIRS, LANE, LANE), F32)],
        compiler_params=_cparams(("parallel", "parallel", "arbitrary")),
        name="wkv_scan",
    )(r, v, a, lwf, kf, bf, r, v, a, lwb, kb, bb)


def _load_vt(v_ref, vt_s):
    for c in range(v_ref.shape[1] // TK):
        vt_s[:, c * TK:(c + 1) * TK] = v_ref[0, c * TK:(c + 1) * TK, :].T


def _key_chunks(n_keys, ctx_only):
    chunks = [(0, CTX)]
    if not ctx_only:
        big = min(KEY_BLOCK, n_keys - CTX)
        chunks += [(CTX + j * big, big) for j in range((n_keys - CTX) // big)]
    return chunks


def _attend(streams, k_ref, vt_s, chunks):
    state = [(jnp.full((1, TM), -1e30, F32), jnp.zeros((1, TM), F32),
              jnp.zeros((rows.stop - rows.start, TM), F32)) for _, _, rows in streams]
    subs = [[(start + t, min(TK, size - t)) for t in range(0, size, TK)] for start, size in chunks]

    def score(c, t):
        st, sz = subs[c][t]
        s = [_dot_nt(k_ref[0, st:st + sz, lanes], q) for q, lanes, _ in streams]
        return s, [jnp.max(x, axis=0, keepdims=True) for x in s]

    cur = [score(0, t) for t in range(len(subs[0]))]
    for c in range(len(chunks)):
        nxt = []
        n_next = len(subs[c + 1]) if c + 1 < len(chunks) else 0
        new_state = []
        for j in range(len(streams)):
            m, l, acc = state[j]
            m_new = functools.reduce(jnp.maximum, [mx[j] for _, mx in cur], m)
            alpha = jnp.exp2(m - m_new)
            new_state.append((m_new, alpha * l, alpha * acc))
        for t, (st, sz) in enumerate(subs[c]):
            if t < n_next:
                nxt.append(score(c + 1, t))
            for j, (_, _, rows) in enumerate(streams):
                m_new, l, acc = new_state[j]
                p = jnp.exp2(cur[t][0][j] - m_new)
                new_state[j] = (m_new, l + jnp.sum(p, axis=0, keepdims=True),
                                acc + _dot(vt_s[rows, st:st + sz], p))
        nxt += [score(c + 1, t) for t in range(len(subs[c]), n_next)]
        state = new_state
        cur = nxt
    return [(acc, l) for _, l, acc in state]


DIFF_HEADS = 2


def _diff_attn_kernel(lam_init, q_ref, k_ref, v_ref, lamv_ref, g_ref, o_ref, vt_s):
    i = pl.program_id(2)

    @pl.when(i == 0)
    def _():
        _load_vt(v_ref, vt_s)

    lv = lamv_ref[...]
    lam = (jnp.exp(jnp.sum(lv[0:1] * lv[1:2], axis=-1, keepdims=True))
           - jnp.exp(jnp.sum(lv[2:3] * lv[3:4], axis=-1, keepdims=True)) + lam_init)
    lane = lax.broadcasted_iota(jnp.int32, (TM, LANE), 1)
    zq = jnp.zeros((TM, LANE), F32)
    heads = [slice(h * LANE, (h + 1) * LANE) for h in range(DIFF_HEADS)]
    streams = []
    for lanes in heads:
        q = q_ref[0, :, lanes] * (DIFF_HD ** -0.5 * LOG2E)
        streams += [(jnp.where(lane < DIFF_HD, q, zq), lanes, lanes), (jnp.where(lane < DIFF_HD, zq, q), lanes, lanes)]

    def run(ctx_only):
        res = _attend(streams, k_ref, vt_s, _key_chunks(k_ref.shape[1], ctx_only))
        for h, lanes in enumerate(heads):
            (a1, l1), (a2, l2) = res[2 * h], res[2 * h + 1]
            o = a1 / l1 - lam * (a2 / l2)
            ms = jnp.mean(o * o, axis=0, keepdims=True)
            o = o * lax.rsqrt(ms + NORM_EPS)
            o_ref[0, :, lanes] = o.T * g_ref[...] * (1.0 - lam_init)

    pl.when(i == 0)(functools.partial(run, True))
    pl.when(i > 0)(functools.partial(run, False))


def _diff_attention(zb, lam_vecs, subln_g, layer_idx):
    B, T, _ = zb.shape
    nt = T // TM
    lam_init = 0.8 - 0.6 * math.exp(-0.3 * layer_idx)
    width = DIFF_HEADS * LANE
    groups = H_B // DIFF_HEADS
    return pl.pallas_call(
        functools.partial(_diff_attn_kernel, lam_init),
        grid=(B, groups, nt),
        in_specs=[
            pl.BlockSpec((1, TM, width), lambda b, h, i: (b, i, h)),
            pl.BlockSpec((1, T, width), lambda b, h, i: (b, 0, groups + h)),
            pl.BlockSpec((1, T, width), lambda b, h, i: (b, 0, 2 * groups + h)),
            pl.BlockSpec((4, DIFF_HD), lambda b, h, i: (0, 0)),
            pl.BlockSpec((1, LANE), lambda b, h, i: (0, 0)),
        ],
        out_specs=pl.BlockSpec((1, TM, width), lambda b, h, i: (b, i, h)),
        out_shape=jax.ShapeDtypeStruct((B, T, H_B * LANE), F32),
        scratch_shapes=[pltpu.VMEM((width, T), F32)],
        compiler_params=_cparams(("parallel", "parallel", "arbitrary")),
        name="diff_attn",
    )(zb, zb, zb, lam_vecs, subln_g)


def _mla_proj_kernel(z_ref, qg_ref, kvg_ref, wq_ref, wqr_ref, wk_ref, wv_ref, cq_ref, sq_ref, ck_ref, sk_ref,
                     q_o, k_o, v_o):
    z = z_ref[0]
    cq = z[:, 0:Q_LORA]
    ckv = z[:, Q_LORA:Q_LORA + KV_LORA]
    kr = z[:, 5 * LANE:6 * LANE] * ck_ref[...] + z[:, 6 * LANE:7 * LANE] * sk_ref[...]
    rms = lambda t: t * lax.rsqrt(jnp.mean(t * t, axis=-1, keepdims=True) + NORM_EPS)
    cqn = (rms(cq) * qg_ref[...]).astype(BF16)
    ckvn = rms(ckv) * kvg_ref[...]
    q = _dot(cqn, wq_ref[...])
    qr = _dot(cqn, wqr_ref[...])
    cos = cq_ref[...]
    sin = sq_ref[...]
    scale = (MLA_NOPE + MLA_ROPE) ** -0.5 * LOG2E
    for h in range(H_C):
        sl = slice(h * LANE, (h + 1) * LANE)
        q_o[0, :, sl] = (q[:, sl] * cos + qr[:, sl] * sin) * scale
    kin = jnp.concatenate([ckvn, kr], axis=1).astype(BF16)
    k_o[0] = _dot(kin, wk_ref[...])
    v_o[0] = _dot(ckvn.astype(BF16), wv_ref[...])


def _mla_project(zc, p):
    B, T, _ = zc.shape
    full = lambda a: pl.BlockSpec(a.shape, lambda b, i: (0,) * a.ndim)
    tab = pl.BlockSpec((TM, LANE), lambda b, i: (i, 0))
    ws = [p["q_g"], p["kv_g"], p["w_uq"], p["w_uq_rot"], p["w_uk"], p["w_uv"]]
    return pl.pallas_call(
        _mla_proj_kernel,
        grid=(B, T // TM),
        in_specs=[pl.BlockSpec((1, TM, MLA_PAD), lambda b, i: (b, i, 0))] + [full(a) for a in ws] + [tab] * 4,
        out_specs=[
            pl.BlockSpec((1, TM, H_C * LANE), lambda b, i: (b, i, 0)),
            pl.BlockSpec((1, TM, H_C * LANE), lambda b, i: (b, i, 0)),
            pl.BlockSpec((1, TM, H_C * MLA_V), lambda b, i: (b, i, 0)),
        ],
        out_shape=[
            jax.ShapeDtypeStruct((B, T, H_C * LANE), F32),
            jax.ShapeDtypeStruct((B, T, H_C * LANE), F32),
            jax.ShapeDtypeStruct((B, T, H_C * MLA_V), F32),
        ],
        compiler_params=_cparams(("parallel", "parallel")),
        name="mla_project",
    )(zc, *ws, p["cos_q"], p["sin_q"], p["cos_k"], p["sin_k"])


MLA_HEADS = 4


def _mla_attn_kernel(q_ref, k_ref, v_ref, o_ref, vt_s):
    i = pl.program_id(2)

    @pl.when(i == 0)
    def _():
        _load_vt(v_ref, vt_s)

    streams = [(q_ref[0, :, h * LANE:(h + 1) * LANE], slice(h * LANE, (h + 1) * LANE),
                slice(h * MLA_V, (h + 1) * MLA_V)) for h in range(MLA_HEADS)]

    def run(ctx_only):
        res = _attend(streams, k_ref, vt_s, _key_chunks(k_ref.shape[1], ctx_only))
        for p in range(MLA_HEADS // 2):
            (a0, l0), (a1, l1) = res[2 * p], res[2 * p + 1]
            o_ref[0, :, p * LANE:(p + 1) * LANE] = jnp.concatenate([a0 / l0, a1 / l1], axis=0).T

    pl.when(i == 0)(functools.partial(run, True))
    pl.when(i > 0)(functools.partial(run, False))


def _mla_attention(q, k, v):
    B, T, _ = q.shape
    nt = T // TM
    return pl.pallas_call(
        _mla_attn_kernel,
        grid=(B, H_C // MLA_HEADS, nt),
        in_specs=[
            pl.BlockSpec((1, TM, MLA_HEADS * LANE), lambda b, p, i: (b, i, p)),
            pl.BlockSpec((1, T, MLA_HEADS * LANE), lambda b, p, i: (b, 0, p)),
            pl.BlockSpec((1, T, MLA_HEADS * MLA_V), lambda b, p, i: (b, 0, p)),
        ],
        out_specs=pl.BlockSpec((1, TM, MLA_HEADS * MLA_V), lambda b, p, i: (b, i, p)),
        out_shape=jax.ShapeDtypeStruct((B, T, H_C * MLA_V), F32),
        scratch_shapes=[pltpu.VMEM((MLA_HEADS * MLA_V, T), F32)],
        compiler_params=_cparams(("parallel", "parallel", "arbitrary")),
        name="mla_attn",
    )(q, k, v)


def _merge_kernel(x_ref, yf_ref, yb_ref, bonus_ref, g_ref, yd_ref, yc_ref, zg_ref, g1_ref, lnw_ref, lnb_ref,
                  ones_ref, wb_ref, wo_ref, o_ref):
    ones_bd = ones_ref[...]
    y = yf_ref[0] + yb_ref[0]
    mean = _seg_sum(y, ones_bd) * (1.0 / N_A)
    yc = y - mean
    var = _seg_sum(yc * yc, ones_bd) * (1.0 / N_A)
    ya = (yc * lax.rsqrt(var + GN_EPS) * lnw_ref[...] + lnb_ref[...] + bonus_ref[0]) * g_ref[0]
    acc = None
    for j, yj in enumerate((ya, yd_ref[0], yc_ref[0])):
        gate = _sigmoid(zg_ref[0, :, j * D:(j + 1) * D])
        t = gate * _dot(yj.astype(BF16), wb_ref[j])
        acc = t if acc is None else acc + t
    o_ref[0] = x_ref[0] + g1_ref[0, 0] * _dot(acc.astype(BF16), wo_ref[...])


def _merge(x_all, yf, yb, bonus, g, yd, yc, zg, g1, lnw, lnb, ones_bd, wb, wo, latent_only):
    B, T, _ = x_all.shape
    skip = 1 if latent_only else 0
    row = lambda n: pl.BlockSpec((1, TM, n), lambda b, i: (b, i + skip, 0))
    full = lambda a: pl.BlockSpec(a.shape, lambda b, i: (0,) * a.ndim)
    return pl.pallas_call(
        _merge_kernel,
        grid=(B, T // TM - skip),
        in_specs=[row(D)] + [row(RW)] * 6 + [row(3 * D),
                  pl.BlockSpec((1, 1, 1, D), lambda b, i: (b, _is_ctx(i + skip), 0, 0)),
                  full(lnw), full(lnb), full(ones_bd), full(wb), full(wo)],
        out_specs=pl.BlockSpec((1, TM, D), lambda b, i: (b, i, 0)),
        out_shape=jax.ShapeDtypeStruct((B, T - skip * TM, D), F32),
        compiler_params=_cparams(("parallel", "parallel")),
        name="merge",
    )(x_all, yf, yb, bonus, g, yd, yc, zg, g1, lnw, lnb, ones_bd, wb, wo)


def _ffn_kernel(final, x_ref, g_ref, mod_ref, wg_ref, wu_ref, wd_ref, fg_ref, o_ref):
    x = x_ref[0]
    h = _norm_mod(x, g_ref[...], mod_ref[0, 0, 0:1, :], mod_ref[0, 0, 1:2, :]).astype(BF16)
    gate = _dot(h, wg_ref[...])
    up = _dot(h, wu_ref[...])
    act = (gate * _sigmoid(gate) * up).astype(BF16)
    out = x + mod_ref[0, 0, 2:3, :] * _dot(act, wd_ref[...])
    if final:
        ms = jnp.mean(out * out, axis=-1, keepdims=True)
        out = out * lax.rsqrt(ms + NORM_EPS) * fg_ref[...]
    o_ref[0] = out


def _ffn(x_all, g, mod, wg, wu, wd, fg, final):
    B, T, _ = x_all.shape
    full = lambda a: pl.BlockSpec(a.shape, lambda b, i: (0,) * a.ndim, pipeline_mode=pl.Buffered(1))
    return pl.pallas_call(
        functools.partial(_ffn_kernel, final),
        grid=(B, T // TM),
        in_specs=[
            pl.BlockSpec((1, TM, D), lambda b, i: (b, i, 0)),
            pl.BlockSpec((1, D), lambda b, i: (0, 0)),
            pl.BlockSpec((1, 1, 3, D), lambda b, i: (b, 0 if final else _is_ctx(i), 0, 0)),
            full(wg), full(wu), full(wd),
            pl.BlockSpec((1, D), lambda b, i: (0, 0)),
        ],
        out_specs=pl.BlockSpec((1, TM, D), lambda b, i: (b, i, 0)),
        out_shape=jax.ShapeDtypeStruct((B, T, D), F32),
        compiler_params=_cparams(("parallel", "parallel")),
        name="ffn",
    )(x_all, g, mod, wg, wu, wd, fg)


def _rot_cols(w, half):
    d, n = w.shape
    w4 = w.reshape(d, n // (2 * half), 2, half)
    return jnp.concatenate([-w4[:, :, 1:2], w4[:, :, 0:1]], axis=2).reshape(d, n)


def _rope_tables(n_lat):
    rows = n_lat // GRID_W
    row = jnp.repeat(jnp.arange(rows, dtype=F32), GRID_W)
    col = jnp.tile(jnp.arange(GRID_W, dtype=F32), rows)

    def tables(rot_dim):
        n_freq = rot_dim // 4
        inv_freq = ROPE_BASE ** (-jnp.arange(n_freq, dtype=F32) / n_freq)
        ang = jnp.concatenate([row[:, None] * inv_freq, col[:, None] * inv_freq], axis=-1)
        cos = jnp.concatenate([jnp.ones((CTX, rot_dim // 2), F32), jnp.cos(ang)], axis=0)
        sin = jnp.concatenate([jnp.zeros((CTX, rot_dim // 2), F32), jnp.sin(ang)], axis=0)
        return cos, sin

    T = CTX + n_lat
    cos_b, sin_b = tables(DIFF_HD)
    cos_c, sin_c = tables(MLA_ROPE)
    one = lambda n: jnp.ones((T, n), F32)
    zero = lambda n: jnp.zeros((T, n), F32)
    return dict(
        cos_b=jnp.tile(cos_b, (1, 4)), sin_b=jnp.tile(sin_b, (1, 4)),
        cos_q=jnp.concatenate([one(MLA_NOPE), cos_c, cos_c, one(32)], axis=1),
        sin_q=jnp.concatenate([zero(MLA_NOPE), sin_c, sin_c, zero(32)], axis=1),
        cos_k=jnp.concatenate([cos_c, cos_c, one(96)], axis=1),
        sin_k=jnp.concatenate([sin_c, sin_c, zero(96)], axis=1),
    )


def _pad_heads(w, per_head, n_heads):
    d = w.shape[0]
    w3 = w.reshape(d, n_heads, per_head)
    return jnp.pad(w3, ((0, 0), (0, 0), (0, LANE - per_head))).reshape(d, n_heads * LANE)


def _layer_weights(l, ws):
    (w_in, mu, w0, w2, a0, a2, g2, k_k, k_a, r_k, ln_w, ln_b, q_g, kv_g, w_uq, w_uk, w_uv, w_branch, w_out,
     wg, wu, wd) = [w[l] for w in ws]
    o1, o2, o3 = RWKV_COLS, RWKV_COLS + DIFF_COLS, RWKV_COLS + DIFF_COLS + MLA_COLS
    w_a, w_b, w_c, w_g = w_in[:, :o1], w_in[:, o1:o2], w_in[:, o2:o3], w_in[:, o3:]
    w_kr = w_c[:, Q_LORA + KV_LORA:]
    w_c_ext = jnp.concatenate([w_c, jnp.zeros((D, 96), F32), _rot_cols(w_kr, MLA_ROPE // 2),
                               jnp.zeros((D, 96), F32)], axis=1)
    zpad = jnp.zeros((2, 64, RW), F32)
    seg = jnp.arange(RW) // N_A
    w_uq_p = _pad_heads(w_uq, MLA_NOPE + MLA_ROPE, H_C)
    w_uq_rope = w_uq.reshape(Q_LORA, H_C, MLA_NOPE + MLA_ROPE)[:, :, MLA_NOPE:].reshape(Q_LORA, H_C * MLA_ROPE)
    w_uq_rot = _rot_cols(w_uq_rope, MLA_ROPE // 2).reshape(Q_LORA, H_C, MLA_ROPE)
    w_uq_rot_p = jnp.pad(w_uq_rot, ((0, 0), (0, 0), (MLA_NOPE, LANE - MLA_NOPE - MLA_ROPE))).reshape(Q_LORA, H_C * LANE)
    place = jnp.zeros((LANE, H_C, LANE), F32).at[
        jnp.arange(MLA_ROPE)[:, None], jnp.arange(H_C)[None, :], MLA_NOPE + jnp.arange(MLA_ROPE)[:, None]].set(1.0)
    w_uk_p = jnp.concatenate([_pad_heads(w_uk, MLA_NOPE, H_C), place.reshape(LANE, H_C * LANE)], axis=0)
    return dict(
        w_in=jnp.concatenate([w_a, w_c_ext, w_g, w_b, _rot_cols(w_b[:, :N_ROPE], DIFF_HD // 2)], axis=1).astype(BF16),
        rwkv=dict(
            mu=mu, w0=w0, a0=a0,
            w2=jnp.stack([jnp.concatenate([w2[0], zpad[0]], 0), jnp.concatenate([zpad[1], w2[1]], 0)]).astype(BF16),
            a2=jnp.stack([jnp.concatenate([a2[0], zpad[0]], 0), jnp.concatenate([zpad[1], a2[1]], 0)]).astype(BF16),
            g2=g2.astype(BF16), k_k=k_k[None], k_a=k_a[None], r_k=r_k.reshape(1, RW),
            ones_bd=(seg[:, None] == seg[None, :]).astype(BF16)),
        ln_w=ln_w[None], ln_b=ln_b[None],
        mla=dict(q_g=q_g[None], kv_g=kv_g[None], w_uq=w_uq_p.astype(BF16), w_uq_rot=w_uq_rot_p.astype(BF16),
                 w_uk=w_uk_p.astype(BF16), w_uv=w_uv.astype(BF16)),
        w_branch=w_branch.astype(BF16), w_out=w_out.astype(BF16),
        wg=wg.astype(BF16), wu=wu.astype(BF16), wd=wd.astype(BF16),
    )


def _mod_rows(m, idx, B):
    lat = m[:B][:, idx]
    ctx = jnp.broadcast_to(m[B][idx][None], lat.shape)
    return jnp.stack([lat, ctx], axis=1)


def kernel(x, c, ctx, c_ctx, ada_w, ada_b, norm1_g, norm2_g, w_in, rwkv_shift_mu, rwkv_w0, rwkv_w2, rwkv_a0,
           rwkv_a2, rwkv_g2, rwkv_k_k, rwkv_k_a, rwkv_r_k, rwkv_ln_w, rwkv_ln_b, diff_lambda, diff_subln_g,
           mla_q_norm_g, mla_kv_norm_g, mla_w_uq, mla_w_uk, mla_w_uv, w_branch, w_out, ffn_w_gate, ffn_w_up,
           ffn_w_down, final_norm_g):
    B, S, _ = x.shape
    L = ada_w.shape[0]
    T = CTX + S
    nt = T // TM
    x_all = jnp.concatenate([ctx, x], axis=1)
    c8 = jnp.concatenate([c, c_ctx[None], jnp.zeros((8 - B - 1, D), F32)], axis=0)
    mod = _modulation(c8, ada_w, ada_b).reshape(L, 8, 6, D)
    tabs = _rope_tables(S)
    ws = (w_in, rwkv_shift_mu, rwkv_w0, rwkv_w2, rwkv_a0, rwkv_a2, rwkv_g2, rwkv_k_k, rwkv_k_a, rwkv_r_k,
          rwkv_ln_w, rwkv_ln_b, mla_q_norm_g, mla_kv_norm_g, mla_w_uq, mla_w_uk, mla_w_uv, w_branch, w_out,
          ffn_w_gate, ffn_w_up, ffn_w_down)
    for l in range(L):
        p = _layer_weights(l, ws)
        mod1 = _mod_rows(mod[l], jnp.array([0, 1]), B)
        g1 = _mod_rows(mod[l], jnp.array([2]), B)
        mod2 = _mod_rows(mod[l], jnp.array([3, 4, 5]), B)
        n1 = norm1_g[l][None]

        za, zc, zg, zb, first8, last8 = _inproj(x_all, n1, mod1, p["w_in"], tabs["cos_b"], tabs["sin_b"])

        zeros = lambda n: jnp.zeros((B, n, RWKV_COLS), F32)
        last_rows, first_rows = last8[:, :, HALO - 1], first8[:, :, 0]
        hprev = jnp.concatenate([zeros(2), last_rows[:, 1:nt - 1]], axis=1)[:, :, None]
        hnext = jnp.concatenate([zeros(1), first_rows[:, 2:nt], zeros(1)], axis=1)[:, :, None]
        r, v, a, lwf, lwb, kf, kb, bf, bb, g, bonus = _rwkv_prepare(za, hprev, hnext, p["rwkv"])
        yf, yb = _wkv_scan(r, v, a, lwf, lwb, kf, kb, bf, bb)

        yd = _diff_attention(zb, diff_lambda[l], diff_subln_g[l][None], l)

        q, k, vm = _mla_project(zc, dict(p["mla"], cos_q=tabs["cos_q"], sin_q=tabs["sin_q"],
                                         cos_k=tabs["cos_k"], sin_k=tabs["sin_k"]))
        yc = _mla_attention(q, k, vm)

        last = l == L - 1
        x_all = _merge(x_all, yf, yb, bonus, g, yd, yc, zg, g1, p["ln_w"], p["ln_b"], p["rwkv"]["ones_bd"],
                       p["w_branch"], p["w_out"], last)
        x_all = _ffn(x_all, norm2_g[l][None], mod2, p["wg"], p["wu"], p["wd"], final_norm_g[None], last)
    return x_all
```

```python
import functools
import math

import jax
import jax.numpy as jnp
from jax import lax
from jax.experimental import pallas as pl
from jax.experimental.pallas import tpu as pltpu

F32 = jnp.float32
BF16 = jnp.bfloat16

D = 1024
CTX = 256
GRID_W = 64
ROPE_BASE = 10000.0
NORM_EPS = 1e-6
GN_EPS = 64e-5

H_A, N_A = 8, 64
RW = H_A * N_A
RWKV_COLS = 3 * RW + 2 * 64 + 2 * 64 + 128
H_B, DIFF_HD = 4, 64
DIFF_QK = H_B * 2 * DIFF_HD
DIFF_COLS = 3 * DIFF_QK
H_C = 8
Q_LORA, KV_LORA = 384, 256
MLA_NOPE, MLA_ROPE, MLA_V = 64, 32, 64
MLA_COLS = Q_LORA + KV_LORA + MLA_ROPE
MLA_PAD = 896
D_FF = 2816

TM = 256
LANE = 128
CHUNK = 64
TK = 256
KEY_BLOCK = 2048
LOG2E = 1.4426950408889634
VMEM_LIMIT = 56 * 1024 * 1024


def _cparams(sem):
    return pltpu.CompilerParams(dimension_semantics=sem, vmem_limit_bytes=VMEM_LIMIT)


def _sigmoid(x):
    return 1.0 / (1.0 + jnp.exp(-x))


def _dot(a, b):
    return jnp.dot(a, b, preferred_element_type=F32)


def _dot_nt(a, b):
    return lax.dot_general(a, b, (((1,), (1,)), ((), ())), preferred_element_type=F32)


def _dot_tn(a, b):
    return lax.dot_general(a, b, (((0,), (0,)), ((), ())), preferred_element_type=F32)


def _seg_sum(x, ones_bd):
    return _dot(x.astype(BF16), ones_bd)


def _is_ctx(i):
    return jnp.where(i == 0, 1, 0)


def _mod_kernel(c_ref, w_ref, b_ref, o_ref):
    c = c_ref[...]
    s = (c * _sigmoid(c)).astype(BF16)
    o_ref[0] = _dot(s, w_ref[0]) + b_ref[0]


def _modulation(c8, ada_w, ada_b):
    L = ada_w.shape[0]
    tn = 1536
    return pl.pallas_call(
        _mod_kernel,
        grid=(L, 6 * D // tn),
        in_specs=[
            pl.BlockSpec((8, D), lambda l, j: (0, 0)),
            pl.BlockSpec((1, D, tn), lambda l, j: (l, 0, j)),
            pl.BlockSpec((1, 1, tn), lambda l, j: (l, 0, j)),
        ],
        out_specs=pl.BlockSpec((1, 8, tn), lambda l, j: (l, 0, j)),
        out_shape=jax.ShapeDtypeStruct((L, 8, 6 * D), F32),
        compiler_params=_cparams(("parallel", "parallel")),
        name="adaln_mod",
    )(c8, ada_w.astype(BF16), ada_b[:, None, :])


def _norm_mod(x, g, shift, scale):
    ms = jnp.mean(x * x, axis=-1, keepdims=True)
    return (x * lax.rsqrt(ms + NORM_EPS)) * g * (1.0 + scale) + shift


N_ROPE = 2 * DIFF_QK
IN_SPLITS = (RWKV_COLS, MLA_PAD, 3 * D, DIFF_COLS)
HALO = 8


def _inproj_kernel(x_ref, g_ref, mod_ref, w_ref, cos_ref, sin_ref, za_o, zc_o, zg_o, zb_o, first_o, last_o):
    h = _norm_mod(x_ref[0], g_ref[...], mod_ref[0, 0, 0:1, :], mod_ref[0, 0, 1:2, :]).astype(BF16)
    ofs = [0]
    for n in IN_SPLITS:
        ofs.append(ofs[-1] + n)
    proj = lambda j: _dot(h, w_ref[:, ofs[j]:ofs[j + 1]])
    za = proj(0)
    za_o[0] = za
    first_o[0, 0] = za[0:HALO]
    last_o[0, 0] = za[TM - HALO:TM]
    zc_o[0] = proj(1)
    zg_o[0] = proj(2)
    z = proj(3)
    cos = cos_ref[...]
    sin = sin_ref[...]
    half = DIFF_HD // 2
    first_half = jnp.bitwise_and(lax.broadcasted_iota(jnp.int32, (TM, LANE), 1), DIFF_HD - 1) < half
    for j in range(N_ROPE // LANE):
        sl = slice(j * LANE, (j + 1) * LANE)
        zj = z[:, sl]
        rot = jnp.where(first_half, -pltpu.roll(zj, LANE - half, 1), pltpu.roll(zj, half, 1))
        zb_o[0, :, sl] = zj * cos + rot * sin
    zb_o[0, :, N_ROPE:] = z[:, N_ROPE:]


def _inproj(x_all, g, mod, w, cos, sin):
    B, T, _ = x_all.shape
    nt = T // TM
    row = lambda n: pl.BlockSpec((1, TM, n), lambda b, i: (b, i, 0))
    edge = pl.BlockSpec((1, 1, HALO, RWKV_COLS), lambda b, i: (b, i, 0, 0))
    widths = (RWKV_COLS, MLA_PAD, 3 * D, DIFF_COLS)
    return pl.pallas_call(
        _inproj_kernel,
        grid=(B, nt),
        in_specs=[
            row(D),
            pl.BlockSpec((1, D), lambda b, i: (0, 0)),
            pl.BlockSpec((1, 1, 2, D), lambda b, i: (b, _is_ctx(i), 0, 0)),
            pl.BlockSpec(w.shape, lambda b, i: (0, 0), pipeline_mode=pl.Buffered(1)),
            pl.BlockSpec((TM, LANE), lambda b, i: (i, 0)),
            pl.BlockSpec((TM, LANE), lambda b, i: (i, 0)),
        ],
        out_specs=[row(n) for n in widths] + [edge, edge],
        out_shape=[jax.ShapeDtypeStruct((B, T, n), F32) for n in widths]
        + [jax.ShapeDtypeStruct((B, nt, HALO, RWKV_COLS), F32)] * 2,
        compiler_params=_cparams(("parallel", "parallel")),
        name="inproj",
    )(x_all, g, mod, w, cos, sin)


RWKV_PARAMS = ("mu", "w0", "w2", "a0", "a2", "g2", "k_k", "k_a", "r_k", "ones_bd")


SCAN_OPERANDS = ("r", "lw", "k", "v", "a", "b")


def _rwkv_tile(z, halo_prev, halo_next, prm, d):
    row = lax.broadcasted_iota(jnp.int32, z.shape, 0)
    prev = jnp.where(row == 0, halo_prev, pltpu.roll(z, 1, 0))
    nxt = jnp.where(row == TM - 1, halo_next, pltpu.roll(z, TM - 1, 0))
    zs = z + prm["mu"][0:1, :] * (prev - z) + prm["mu"][1:2, :] * (nxt - z)
    r = zs[:, 0:RW]
    k = zs[:, RW:2 * RW]
    v = zs[:, 2 * RW:3 * RW]
    wd = zs[:, 3 * RW:3 * RW + LANE]
    ad = zs[:, 3 * RW + LANE:3 * RW + 2 * LANE]
    gd = zs[:, 3 * RW + 2 * LANE:3 * RW + 3 * LANE]
    ones_bd = prm["ones_bd"][...]
    kk = k * prm["k_k"][...]
    sumsq = _seg_sum(kk * kk, ones_bd)
    u = _dot(jnp.tanh(wd).astype(BF16), prm["w2"][d])
    alr = _dot(ad.astype(BF16), prm["a2"][d])
    kk = kk / jnp.maximum(jnp.sqrt(sumsq), 1e-12)
    alr = _sigmoid(prm["a0"][d:d + 1, :] + alr)
    kd = k * (1.0 + (alr - 1.0) * prm["k_a"][...])
    return dict(
        r=r, v=v, a=-kk, k=kd, b=kk * alr, gd=gd,
        lw=-math.exp(-0.5) * _sigmoid(prm["w0"][d:d + 1, :] + u),
        bonus=_seg_sum(r * prm["r_k"][...] * kd, ones_bd) * v,
    )


def _wkv_consts(rev):
    t = lax.broadcasted_iota(jnp.int32, (CHUNK, LANE), 0)
    lane = lax.broadcasted_iota(jnp.int32, (CHUNK, LANE), 1)
    s = jnp.bitwise_and(lane, CHUNK - 1)
    tt = lax.broadcasted_iota(jnp.int32, (CHUNK, CHUNK), 0)
    ss = lax.broadcasted_iota(jnp.int32, (CHUNK, CHUNK), 1)
    r2 = lax.broadcasted_iota(jnp.int32, (LANE, LANE), 0)
    l2 = lax.broadcasted_iota(jnp.int32, (LANE, LANE), 1)
    return dict(
        strict=(s > t) if rev else (s < t),
        incl=(s >= t) if rev else (s <= t),
        eye=jnp.where(s == t, 1.0, 0.0).astype(F32),
        m0=lane < CHUNK,
        tri=jnp.where((ss >= tt) if rev else (ss <= tt), 1.0, 0.0).astype(F32),
        bdm=(r2 >= CHUNK) == (l2 >= CHUNK),
        ones=jnp.ones((LANE, LANE), F32),
        levels=[((t >> (j + 1)) == (s >> (j + 1))) & ((t >> j) != (s >> j)) for j in range(int(math.log2(CHUNK)))],
    )


def _bd(y, m0):
    z = jnp.zeros_like(y)
    return jnp.concatenate([jnp.where(m0, y, z), jnp.where(m0, z, y)], axis=0)


def _wkv_prepare_chunks(chunks):
    C = CHUNK
    zero = jnp.zeros((C, LANE), F32)
    n = range(len(chunks))
    cs = [ch[6] for ch in chunks]
    m0 = [c["m0"] for c in cs]
    lw = [ch[1] for ch in chunks]
    v = [ch[3] for ch in chunks]
    lw_hi = [x.astype(BF16).astype(F32) for x in lw]
    lw_lo = [lw[i] - lw_hi[i] for i in n]
    cc = [_dot(cs[i]["tri"], jnp.concatenate([lw_hi[i], lw_lo[i]], axis=1)) for i in n]
    tcol = [_dot_tn(jnp.concatenate([lw_hi[i], lw_lo[i]], axis=0), cs[i]["ones"]) for i in n]
    cum =[x[:, :LANE] + x[:, LANE:] for x in cc]
    total = [cum[i][0:1] if chunks[i][7] else cum[i][C - 1:C] for i in n]
    p_inc = [jnp.exp(x) for x in cum]
    p_prev = [jnp.exp(cum[i] - lw[i]) for i in n]
    p_inv = [jnp.exp(-x) for x in cum]
    p_sc = [jnp.exp(total[i] - cum[i]) for i in n]
    rt = [chunks[i][0] * p_inc[i] for i in n]
    at = [chunks[i][4] * p_prev[i] for i in n]
    bt = [chunks[i][5] * p_inv[i] for i in n]
    kt = [chunks[i][2] * p_inv[i] for i in n]
    bks = [jnp.concatenate([chunks[i][5] * p_sc[i], chunks[i][2] * p_sc[i]], axis=0) for i in n]
    pw = [_dot_nt(jnp.concatenate([at[i], rt[i]], axis=0),
                  jnp.concatenate([_bd(bt[i], m0[i]), _bd(kt[i], m0[i])], axis=0)) for i in n]
    ab = [jnp.where(cs[i]["strict"], pw[i][:C, :LANE], zero) for i in n]
    akrk = [jnp.concatenate([jnp.where(cs[i]["strict"], pw[i][:C, LANE:], zero),
                             jnp.where(cs[i]["incl"], pw[i][C:, LANE:], zero)], axis=0) for i in n]
    rb = [jnp.where(cs[i]["incl"], pw[i][C:, :LANE], zero) for i in n]
    tinv = [cs[i]["eye"] + jnp.where(cs[i]["levels"][0], ab[i], zero) for i in n]
    for j in range(1, len(cs[0]["levels"])):
        tl = [_dot(tinv[i], _bd(jnp.where(cs[i]["levels"][j], ab[i], zero), m0[i])) for i in n]
        tinv = [tinv[i] + _dot(tl[i], _bd(tinv[i], m0[i])) for i in n]
    akv = [_dot(akrk[i], _bd(v[i], m0[i])) for i in n]
    wx = [_dot(tinv[i], jnp.concatenate([_bd(at[i], m0[i]), _bd(akv[i][:C], m0[i])], axis=1)) for i in n]
    return [dict(wr=jnp.concatenate([wx[i][:, :LANE], rt[i]], axis=0), x=wx[i][:, LANE:], yk=akv[i][C:], rb=rb[i],
                 bks=bks[i], v=v[i], pcol=jnp.exp(tcol[i])) for i in n]


WKV_PAIRS = RW // LANE


def _wkv_kernel(zf_ref, zb_ref, hpf_ref, hnf_ref, hpb_ref, hnb_ref, *rest):
    prm = dict(zip(RWKV_PARAMS, rest))
    yf_o, yb_o, bonf_o, bonb_o, g_o, h_s = rest[len(RWKV_PARAMS):]

    @pl.when(pl.program_id(1) == 0)
    def _():
        h_s[...] = jnp.zeros_like(h_s)

    C = CHUNK
    nch = TM // C
    tiles = (_rwkv_tile(zf_ref[0], hpf_ref[0, 0], hnf_ref[0, 0], prm, 0),
             _rwkv_tile(zb_ref[0], hpb_ref[0, 0], hnb_ref[0, 0], prm, 1))
    bonf_o[0] = tiles[0]["bonus"]
    bonb_o[0] = tiles[1]["bonus"]
    g_o[0] = _dot(_sigmoid(tiles[0]["gd"]).astype(BF16), prm["g2"][...])
    dirs = ((False, tiles[0], yf_o), (True, tiles[1], yb_o))
    order = {False: list(range(nch)), True: list(range(nch - 1, -1, -1))}
    chains = [(rev, tile, y_o, slice(p * LANE, (p + 1) * LANE)) for rev, tile, y_o in dirs for p in range(WKV_PAIRS)]
    consts = {rev: _wkv_consts(rev) for rev, _, _ in dirs}
    chunks = []
    for rev, tile, _, lanes in chains:
        for c in order[rev]:
            chunks.append(tuple(tile[name][c * C:(c + 1) * C, lanes] for name in SCAN_OPERANDS) + (consts[rev], rev))
    prepared = _wkv_prepare_chunks(chunks)
    m0 = consts[False]["m0"]
    bdm = consts[False]["bdm"]
    nc = range(len(chains))
    states = [h_s[j] for j in nc]
    for step in range(nch):
        ps = [prepared[j * nch + step] for j in nc]
        s1 = [_dot(ps[j]["wr"], states[j]) for j in nc]
        u = [s1[j][:C] + ps[j]["x"] for j in nc]
        s2 = [_dot(ps[j]["rb"], _bd(u[j], m0)) for j in nc]
        s3 = [_dot_tn(ps[j]["bks"], jnp.concatenate([u[j], ps[j]["v"]], axis=0)) for j in nc]
        for j, (rev, _, y_o, lanes) in enumerate(chains):
            c = order[rev][step]
            y_o[0, c * C:(c + 1) * C, lanes] = s1[j][C:] + s2[j] + ps[j]["yk"]
        states = [ps[j]["pcol"] * states[j] + jnp.where(bdm, s3[j], jnp.zeros_like(s3[j])) for j in nc]
    for j in nc:
        h_s[j] = states[j]


def _wkv_scan(za, hprev, hnext, p):
    B, T, _ = za.shape
    nt = T // TM
    rev_tile = lambda j: jnp.where(j == 0, 0, nt - j)
    tile = lambda n, idx: pl.BlockSpec((1, TM, n), lambda b, i: (b, idx(i), 0))
    halo = lambda idx: pl.BlockSpec((1, 1, 1, RWKV_COLS), lambda b, i: (b, idx(i), 0, 0))
    full = lambda a: pl.BlockSpec(a.shape, lambda b, i: (0,) * a.ndim)
    fwd = lambda i: i
    params = [p[name] for name in RWKV_PARAMS]
    return pl.pallas_call(
        _wkv_kernel,
        grid=(B, nt),
        in_specs=[tile(RWKV_COLS, fwd), tile(RWKV_COLS, rev_tile), halo(fwd), halo(fwd), halo(rev_tile),
                  halo(rev_tile)] + [full(a) for a in params],
        out_specs=[tile(RW, fwd), tile(RW, rev_tile), tile(RW, fwd), tile(RW, rev_tile), tile(RW, fwd)],
        out_shape=[jax.ShapeDtypeStruct((B, T, RW), F32)] * 5,
        scratch_shapes=[pltpu.VMEM((2 * WKV_PAIRS, LANE, LANE), F32)],
        compiler_params=_cparams(("parallel", "arbitrary")),
        name="wkv_scan",
    )(za, za, hprev, hnext, hprev, hnext, *params)


def _load_vt(v_ref, vt_s):
    for c in range(v_ref.shape[1] // TK):
        vt_s[:, c * TK:(c + 1) * TK] = v_ref[0, c * TK:(c + 1) * TK, :].T.astype(BF16)


def _key_chunks(n_keys, ctx_only):
    chunks = [(0, CTX)]
    if not ctx_only:
        big = min(KEY_BLOCK, n_keys - CTX)
        chunks += [(CTX + j * big, big) for j in range((n_keys - CTX) // big)]
    return chunks


def _attend(streams, k_ref, vt_s, chunks):
    state = [(jnp.full((1, TM), -1e30, F32), jnp.zeros((1, TM), F32),
              jnp.zeros((rows.stop - rows.start, TM), F32)) for _, _, rows in streams]
    subs = [[(start + t, min(TK, size - t)) for t in range(0, size, TK)] for start, size in chunks]

    ns = range(len(streams))

    def score(c, t, j):
        st, sz = subs[c][t]
        q, lanes, _ = streams[j]
        s = _dot_nt(k_ref[0, st:st + sz, lanes], q)
        return s, jnp.max(s, axis=0, keepdims=True)

    cur = [[score(0, t, j) for j in ns] for t in range(len(subs[0]))]
    for c in range(len(chunks)):
        n_next = len(subs[c + 1]) if c + 1 < len(chunks) else 0
        nxt = [[] for _ in range(n_next)]
        new_state = []
        for j in ns:
            m, l, acc = state[j]
            m_new = functools.reduce(jnp.maximum, [row[j][1] for row in cur], m)
            alpha = jnp.exp2(m - m_new)
            new_state.append((m_new, alpha * l, alpha * acc))
        for t, (st, sz) in enumerate(subs[c]):
            for j, (_, _, rows) in enumerate(streams):
                if t < n_next:
                    nxt[t].append(score(c + 1, t, j))
                m_new, l, acc = new_state[j]
                p = jnp.exp2(cur[t][j][0] - m_new)
                new_state[j] = (m_new, l + jnp.sum(p, axis=0, keepdims=True),
                                acc + _dot(vt_s[rows, st:st + sz], p.astype(BF16)))
        for t in range(len(subs[c]), n_next):
            nxt[t] = [score(c + 1, t, j) for j in ns]
        state = new_state
        cur = nxt
    return [(acc, l) for _, l, acc in state]


DIFF_HEADS = 2


def _diff_attn_kernel(lam_init, q_ref, k_ref, v_ref, lamv_ref, g_ref, o_ref, vt_s):
    i = pl.program_id(2)

    @pl.when(i == 0)
    def _():
        _load_vt(v_ref, vt_s)

    lv = lamv_ref[...]
    lam = (jnp.exp(jnp.sum(lv[0:1] * lv[1:2], axis=-1, keepdims=True))
           - jnp.exp(jnp.sum(lv[2:3] * lv[3:4], axis=-1, keepdims=True)) + lam_init)
    lane = lax.broadcasted_iota(jnp.int32, (TM, LANE), 1)
    zq = jnp.zeros((TM, LANE), F32)
    heads = [slice(h * LANE, (h + 1) * LANE) for h in range(DIFF_HEADS)]
    streams = []
    for lanes in heads:
        q = q_ref[0, :, lanes] * (DIFF_HD ** -0.5 * LOG2E)
        streams += [(jnp.where(lane < DIFF_HD, q, zq), lanes, lanes), (jnp.where(lane < DIFF_HD, zq, q), lanes, lanes)]

    def run(ctx_only):
        res = _attend(streams, k_ref, vt_s, _key_chunks(k_ref.shape[1], ctx_only))
        for h, lanes in enumerate(heads):
            (a1, l1), (a2, l2) = res[2 * h], res[2 * h + 1]
            o = a1 / l1 - lam * (a2 / l2)
            ms = jnp.mean(o * o, axis=0, keepdims=True)
            o = o * lax.rsqrt(ms + NORM_EPS)
            o_ref[0, :, lanes] = o.T * g_ref[...] * (1.0 - lam_init)

    pl.when(i == 0)(functools.partial(run, True))
    pl.when(i > 0)(functools.partial(run, False))


def _diff_attention(zb, lam_vecs, subln_g, layer_idx):
    B, T, _ = zb.shape
    nt = T // TM
    lam_init = 0.8 - 0.6 * math.exp(-0.3 * layer_idx)
    width = DIFF_HEADS * LANE
    groups = H_B // DIFF_HEADS
    return pl.pallas_call(
        functools.partial(_diff_attn_kernel, lam_init),
        grid=(B, groups, nt),
        in_specs=[
            pl.BlockSpec((1, TM, width), lambda b, h, i: (b, i, h)),
            pl.BlockSpec((1, T, width), lambda b, h, i: (b, 0, groups + h)),
            pl.BlockSpec((1, T, width), lambda b, h, i: (b, 0, 2 * groups + h)),
            pl.BlockSpec((4, DIFF_HD), lambda b, h, i: (0, 0)),
            pl.BlockSpec((1, LANE), lambda b, h, i: (0, 0)),
        ],
        out_specs=pl.BlockSpec((1, TM, width), lambda b, h, i: (b, i, h)),
        out_shape=jax.ShapeDtypeStruct((B, T, H_B * LANE), F32),
        scratch_shapes=[pltpu.VMEM((width, T), BF16)],
        compiler_params=_cparams(("parallel", "parallel", "arbitrary")),
        name="diff_attn",
    )(zb, zb, zb, lam_vecs, subln_g)


def _mla_proj_kernel(z_ref, qg_ref, kvg_ref, wq_ref, wqr_ref, wk_ref, wv_ref, cq_ref, sq_ref, ck_ref, sk_ref,
                     q_o, k_o, v_o):
    z = z_ref[0]
    cq = z[:, 0:Q_LORA]
    ckv = z[:, Q_LORA:Q_LORA + KV_LORA]
    kr = z[:, 5 * LANE:6 * LANE] * ck_ref[...] + z[:, 6 * LANE:7 * LANE] * sk_ref[...]
    rms = lambda t: t * lax.rsqrt(jnp.mean(t * t, axis=-1, keepdims=True) + NORM_EPS)
    cqn = (rms(cq) * qg_ref[...]).astype(BF16)
    ckvn = rms(ckv) * kvg_ref[...]
    q = _dot(cqn, wq_ref[...])
    qr = _dot(cqn, wqr_ref[...])
    cos = cq_ref[...]
    sin = sq_ref[...]
    scale = (MLA_NOPE + MLA_ROPE) ** -0.5 * LOG2E
    for h in range(H_C):
        sl = slice(h * LANE, (h + 1) * LANE)
        q_o[0, :, sl] = (q[:, sl] * cos + qr[:, sl] * sin) * scale
    kin = jnp.concatenate([ckvn, kr], axis=1).astype(BF16)
    k_o[0] = _dot(kin, wk_ref[...])
    v_o[0] = _dot(ckvn.astype(BF16), wv_ref[...])


def _mla_project(zc, p):
    B, T, _ = zc.shape
    full = lambda a: pl.BlockSpec(a.shape, lambda b, i: (0,) * a.ndim)
    tab = pl.BlockSpec((TM, LANE), lambda b, i: (i, 0))
    ws = [p["q_g"], p["kv_g"], p["w_uq"], p["w_uq_rot"], p["w_uk"], p["w_uv"]]
    return pl.pallas_call(
        _mla_proj_kernel,
        grid=(B, T // TM),
        in_specs=[pl.BlockSpec((1, TM, MLA_PAD), lambda b, i: (b, i, 0))] + [full(a) for a in ws] + [tab] * 4,
        out_specs=[
            pl.BlockSpec((1, TM, H_C * LANE), lambda b, i: (b, i, 0)),
            pl.BlockSpec((1, TM, H_C * LANE), lambda b, i: (b, i, 0)),
            pl.BlockSpec((1, TM, H_C * MLA_V), lambda b, i: (b, i, 0)),
        ],
        out_shape=[
            jax.ShapeDtypeStruct((B, T, H_C * LANE), F32),
            jax.ShapeDtypeStruct((B, T, H_C * LANE), F32),
            jax.ShapeDtypeStruct((B, T, H_C * MLA_V), F32),
        ],
        compiler_params=_cparams(("parallel", "parallel")),
        name="mla_project",
    )(zc, *ws, p["cos_q"], p["sin_q"], p["cos_k"], p["sin_k"])


MLA_HEADS = 4


def _mla_attn_kernel(q_ref, k_ref, v_ref, o_ref, vt_s):
    i = pl.program_id(2)

    @pl.when(i == 0)
    def _():
        _load_vt(v_ref, vt_s)

    streams = [(q_ref[0, :, h * LANE:(h + 1) * LANE], slice(h * LANE, (h + 1) * LANE),
                slice(h * MLA_V, (h + 1) * MLA_V)) for h in range(MLA_HEADS)]

    def run(ctx_only):
        res = _attend(streams, k_ref, vt_s, _key_chunks(k_ref.shape[1], ctx_only))
        for p in range(MLA_HEADS // 2):
            (a0, l0), (a1, l1) = res[2 * p], res[2 * p + 1]
            o_ref[0, :, p * LANE:(p + 1) * LANE] = jnp.concatenate([a0 / l0, a1 / l1], axis=0).T

    pl.when(i == 0)(functools.partial(run, True))
    pl.when(i > 0)(functools.partial(run, False))


def _mla_attention(q, k, v):
    B, T, _ = q.shape
    nt = T // TM
    return pl.pallas_call(
        _mla_attn_kernel,
        grid=(B, H_C // MLA_HEADS, nt),
        in_specs=[
            pl.BlockSpec((1, TM, MLA_HEADS * LANE), lambda b, p, i: (b, i, p)),
            pl.BlockSpec((1, T, MLA_HEADS * LANE), lambda b, p, i: (b, 0, p)),
            pl.BlockSpec((1, T, MLA_HEADS * MLA_V), lambda b, p, i: (b, 0, p)),
        ],
        out_specs=pl.BlockSpec((1, TM, MLA_HEADS * MLA_V), lambda b, p, i: (b, i, p)),
        out_shape=jax.ShapeDtypeStruct((B, T, H_C * MLA_V), F32),
        scratch_shapes=[pltpu.VMEM((MLA_HEADS * MLA_V, T), BF16)],
        compiler_params=_cparams(("parallel", "parallel", "arbitrary")),
        name="mla_attn",
    )(q, k, v)


def _merge_kernel(x_ref, yf_ref, yb_ref, bonf_ref, bonb_ref, g_ref, yd_ref, yc_ref, zg_ref, g1_ref, lnw_ref,
                  lnb_ref, ones_ref, wb_ref, wo_ref, o_ref):
    ones_bd = ones_ref[...]
    y = yf_ref[0] + yb_ref[0]
    mean = _seg_sum(y, ones_bd) * (1.0 / N_A)
    yc = y - mean
    var = _seg_sum(yc * yc, ones_bd) * (1.0 / N_A)
    bonus = bonf_ref[0] + bonb_ref[0]
    ya = (yc * lax.rsqrt(var + GN_EPS) * lnw_ref[...] + lnb_ref[...] + bonus) * g_ref[0]
    acc = None
    for j, yj in enumerate((ya, yd_ref[0], yc_ref[0])):
        gate = _sigmoid(zg_ref[0, :, j * D:(j + 1) * D])
        t = gate * _dot(yj.astype(BF16), wb_ref[j])
        acc = t if acc is None else acc + t
    o_ref[0] = x_ref[0] + g1_ref[0, 0] * _dot(acc.astype(BF16), wo_ref[...])


def _merge(x_all, yf, yb, bonf, bonb, g, yd, yc, zg, g1, lnw, lnb, ones_bd, wb, wo, latent_only):
    B, T, _ = x_all.shape
    skip = 1 if latent_only else 0
    row = lambda n: pl.BlockSpec((1, TM, n), lambda b, i: (b, i + skip, 0))
    full = lambda a: pl.BlockSpec(a.shape, lambda b, i: (0,) * a.ndim)
    return pl.pallas_call(
        _merge_kernel,
        grid=(B, T // TM - skip),
        in_specs=[row(D)] + [row(RW)] * 7 + [row(3 * D),
                  pl.BlockSpec((1, 1, 1, D), lambda b, i: (b, _is_ctx(i + skip), 0, 0)),
                  full(lnw), full(lnb), full(ones_bd), full(wb), full(wo)],
        out_specs=pl.BlockSpec((1, TM, D), lambda b, i: (b, i, 0)),
        out_shape=jax.ShapeDtypeStruct((B, T - skip * TM, D), F32),
        compiler_params=_cparams(("parallel", "parallel")),
        name="merge",
    )(x_all, yf, yb, bonf, bonb, g, yd, yc, zg, g1, lnw, lnb, ones_bd, wb, wo)


def _ffn_kernel(final, x_ref, g_ref, mod_ref, wg_ref, wu_ref, wd_ref, fg_ref, o_ref):
    x = x_ref[0]
    h = _norm_mod(x, g_ref[...], mod_ref[0, 0, 0:1, :], mod_ref[0, 0, 1:2, :]).astype(BF16)
    gate = _dot(h, wg_ref[...])
    up = _dot(h, wu_ref[...])
    act = (gate * _sigmoid(gate) * up).astype(BF16)
    out = x + mod_ref[0, 0, 2:3, :] * _dot(act, wd_ref[...])
    if final:
        ms = jnp.mean(out * out, axis=-1, keepdims=True)
        out = out * lax.rsqrt(ms + NORM_EPS) * fg_ref[...]
    o_ref[0] = out


def _ffn(x_all, g, mod, wg, wu, wd, fg, final):
    B, T, _ = x_all.shape
    full = lambda a: pl.BlockSpec(a.shape, lambda b, i: (0,) * a.ndim, pipeline_mode=pl.Buffered(1))
    return pl.pallas_call(
        functools.partial(_ffn_kernel, final),
        grid=(B, T // TM),
        in_specs=[
            pl.BlockSpec((1, TM, D), lambda b, i: (b, i, 0)),
            pl.BlockSpec((1, D), lambda b, i: (0, 0)),
            pl.BlockSpec((1, 1, 3, D), lambda b, i: (b, 0 if final else _is_ctx(i), 0, 0)),
            full(wg), full(wu), full(wd),
            pl.BlockSpec((1, D), lambda b, i: (0, 0)),
        ],
        out_specs=pl.BlockSpec((1, TM, D), lambda b, i: (b, i, 0)),
        out_shape=jax.ShapeDtypeStruct((B, T, D), F32),
        compiler_params=_cparams(("parallel", "parallel")),
        name="ffn",
    )(x_all, g, mod, wg, wu, wd, fg)


def _rot_cols(w, half):
    d, n = w.shape
    w4 = w.reshape(d, n // (2 * half), 2, half)
    return jnp.concatenate([-w4[:, :, 1:2], w4[:, :, 0:1]], axis=2).reshape(d, n)


def _rope_tables(n_lat):
    rows = n_lat // GRID_W
    row = jnp.repeat(jnp.arange(rows, dtype=F32), GRID_W)
    col = jnp.tile(jnp.arange(GRID_W, dtype=F32), rows)

    def tables(rot_dim):
        n_freq = rot_dim // 4
        inv_freq = ROPE_BASE ** (-jnp.arange(n_freq, dtype=F32) / n_freq)
        ang = jnp.concatenate([row[:, None] * inv_freq, col[:, None] * inv_freq], axis=-1)
        cos = jnp.concatenate([jnp.ones((CTX, rot_dim // 2), F32), jnp.cos(ang)], axis=0)
        sin = jnp.concatenate([jnp.zeros((CTX, rot_dim // 2), F32), jnp.sin(ang)], axis=0)
        return cos, sin

    T = CTX + n_lat
    cos_b, sin_b = tables(DIFF_HD)
    cos_c, sin_c = tables(MLA_ROPE)
    one = lambda n: jnp.ones((T, n), F32)
    zero = lambda n: jnp.zeros((T, n), F32)
    return dict(
        cos_b=jnp.tile(cos_b, (1, 4)), sin_b=jnp.tile(sin_b, (1, 4)),
        cos_q=jnp.concatenate([one(MLA_NOPE), cos_c, cos_c, one(32)], axis=1),
        sin_q=jnp.concatenate([zero(MLA_NOPE), sin_c, sin_c, zero(32)], axis=1),
        cos_k=jnp.concatenate([cos_c, cos_c, one(96)], axis=1),
        sin_k=jnp.concatenate([sin_c, sin_c, zero(96)], axis=1),
    )


def _pad_heads(w, per_head, n_heads):
    d = w.shape[0]
    w3 = w.reshape(d, n_heads, per_head)
    return jnp.pad(w3, ((0, 0), (0, 0), (0, LANE - per_head))).reshape(d, n_heads * LANE)


def _layer_weights(l, ws):
    (w_in, mu, w0, w2, a0, a2, g2, k_k, k_a, r_k, ln_w, ln_b, q_g, kv_g, w_uq, w_uk, w_uv, w_branch, w_out,
     wg, wu, wd) = [w[l] for w in ws]
    o1, o2, o3 = RWKV_COLS, RWKV_COLS + DIFF_COLS, RWKV_COLS + DIFF_COLS + MLA_COLS
    w_a, w_b, w_c, w_g = w_in[:, :o1], w_in[:, o1:o2], w_in[:, o2:o3], w_in[:, o3:]
    w_kr = w_c[:, Q_LORA + KV_LORA:]
    w_c_ext = jnp.concatenate([w_c, jnp.zeros((D, 96), F32), _rot_cols(w_kr, MLA_ROPE // 2),
                               jnp.zeros((D, 96), F32)], axis=1)
    zpad = jnp.zeros((2, 64, RW), F32)
    seg = jnp.arange(RW) // N_A
    w_uq_p = _pad_heads(w_uq, MLA_NOPE + MLA_ROPE, H_C)
    w_uq_rope = w_uq.reshape(Q_LORA, H_C, MLA_NOPE + MLA_ROPE)[:, :, MLA_NOPE:].reshape(Q_LORA, H_C * MLA_ROPE)
    w_uq_rot = _rot_cols(w_uq_rope, MLA_ROPE // 2).reshape(Q_LORA, H_C, MLA_ROPE)
    w_uq_rot_p = jnp.pad(w_uq_rot, ((0, 0), (0, 0), (MLA_NOPE, LANE - MLA_NOPE - MLA_ROPE))).reshape(Q_LORA, H_C * LANE)
    place = jnp.zeros((LANE, H_C, LANE), F32).at[
        jnp.arange(MLA_ROPE)[:, None], jnp.arange(H_C)[None, :], MLA_NOPE + jnp.arange(MLA_ROPE)[:, None]].set(1.0)
    w_uk_p = jnp.concatenate([_pad_heads(w_uk, MLA_NOPE, H_C), place.reshape(LANE, H_C * LANE)], axis=0)
    return dict(
        w_in=jnp.concatenate([w_a, w_c_ext, w_g, w_b], axis=1).astype(BF16),
        rwkv=dict(
            mu=mu, w0=w0, a0=a0,
            w2=jnp.stack([jnp.concatenate([w2[0], zpad[0]], 0), jnp.concatenate([zpad[1], w2[1]], 0)]).astype(BF16),
            a2=jnp.stack([jnp.concatenate([a2[0], zpad[0]], 0), jnp.concatenate([zpad[1], a2[1]], 0)]).astype(BF16),
            g2=g2.astype(BF16), k_k=k_k[None], k_a=k_a[None], r_k=r_k.reshape(1, RW),
            ones_bd=(seg[:, None] == seg[None, :]).astype(BF16)),
        ln_w=ln_w[None], ln_b=ln_b[None],
        mla=dict(q_g=q_g[None], kv_g=kv_g[None], w_uq=w_uq_p.astype(BF16), w_uq_rot=w_uq_rot_p.astype(BF16),
                 w_uk=w_uk_p.astype(BF16), w_uv=w_uv.astype(BF16)),
        w_branch=w_branch.astype(BF16), w_out=w_out.astype(BF16),
        wg=wg.astype(BF16), wu=wu.astype(BF16), wd=wd.astype(BF16),
    )


def _mod_rows(m, idx, B):
    lat = m[:B][:, idx]
    ctx = jnp.broadcast_to(m[B][idx][None], lat.shape)
    return jnp.stack([lat, ctx], axis=1)


def kernel(x, c, ctx, c_ctx, ada_w, ada_b, norm1_g, norm2_g, w_in, rwkv_shift_mu, rwkv_w0, rwkv_w2, rwkv_a0,
           rwkv_a2, rwkv_g2, rwkv_k_k, rwkv_k_a, rwkv_r_k, rwkv_ln_w, rwkv_ln_b, diff_lambda, diff_subln_g,
           mla_q_norm_g, mla_kv_norm_g, mla_w_uq, mla_w_uk, mla_w_uv, w_branch, w_out, ffn_w_gate, ffn_w_up,
           ffn_w_down, final_norm_g):
    B, S, _ = x.shape
    L = ada_w.shape[0]
    T = CTX + S
    nt = T // TM
    x_all = jnp.concatenate([ctx, x], axis=1)
    c8 = jnp.concatenate([c, c_ctx[None], jnp.zeros((8 - B - 1, D), F32)], axis=0)
    mod = _modulation(c8, ada_w, ada_b).reshape(L, 8, 6, D)
    tabs = _rope_tables(S)
    ws = (w_in, rwkv_shift_mu, rwkv_w0, rwkv_w2, rwkv_a0, rwkv_a2, rwkv_g2, rwkv_k_k, rwkv_k_a, rwkv_r_k,
          rwkv_ln_w, rwkv_ln_b, mla_q_norm_g, mla_kv_norm_g, mla_w_uq, mla_w_uk, mla_w_uv, w_branch, w_out,
          ffn_w_gate, ffn_w_up, ffn_w_down)
    for l in range(L):
        p = _layer_weights(l, ws)
        mod1 = _mod_rows(mod[l], jnp.array([0, 1]), B)
        g1 = _mod_rows(mod[l], jnp.array([2]), B)
        mod2 = _mod_rows(mod[l], jnp.array([3, 4, 5]), B)
        n1 = norm1_g[l][None]

        za, zc, zg, zb, first8, last8 = _inproj(x_all, n1, mod1, p["w_in"], tabs["cos_b"], tabs["sin_b"])

        zeros = lambda n: jnp.zeros((B, n, RWKV_COLS), F32)
        last_rows, first_rows = last8[:, :, HALO - 1], first8[:, :, 0]
        hprev = jnp.concatenate([zeros(2), last_rows[:, 1:nt - 1]], axis=1)[:, :, None]
        hnext = jnp.concatenate([zeros(1), first_rows[:, 2:nt], zeros(1)], axis=1)[:, :, None]
        yf, yb, bonf, bonb, g = _wkv_scan(za, hprev, hnext, p["rwkv"])

        yd = _diff_attention(zb, diff_lambda[l], diff_subln_g[l][None], l)

        q, k, vm = _mla_project(zc, dict(p["mla"], cos_q=tabs["cos_q"], sin_q=tabs["sin_q"],
                                         cos_k=tabs["cos_k"], sin_k=tabs["sin_k"]))
        yc = _mla_attention(q, k, vm)

        last = l == L - 1
        x_all = _merge(x_all, yf, yb, bonf, bonb, g, yd, yc, zg, g1, p["ln_w"], p["ln_b"], p["rwkv"]["ones_bd"],
                       p["w_branch"], p["w_out"], last)
        x_all = _ffn(x_all, norm2_g[l][None], mod2, p["wg"], p["wu"], p["wd"], final_norm_g[None], last)
    return x_all
```

```python
import functools
import math

import jax
import jax.numpy as jnp
from jax import lax
from jax.experimental import pallas as pl
from jax.experimental.pallas import tpu as pltpu

F32 = jnp.float32
BF16 = jnp.bfloat16

D = 1024
CTX = 256
GRID_W = 64
ROPE_BASE = 10000.0
NORM_EPS = 1e-6
GN_EPS = 64e-5

H_A, N_A = 8, 64
RW = H_A * N_A
RWKV_COLS = 3 * RW + 2 * 64 + 2 * 64 + 128
H_B, DIFF_HD = 4, 64
DIFF_QK = H_B * 2 * DIFF_HD
DIFF_COLS = 3 * DIFF_QK
H_C = 8
Q_LORA, KV_LORA = 384, 256
MLA_NOPE, MLA_ROPE, MLA_V = 64, 32, 64
MLA_COLS = Q_LORA + KV_LORA + MLA_ROPE
MLA_PAD = 896
D_FF = 2816

TM = 256
LANE = 128
CHUNK = 64
TK = 256
KEY_BLOCK = 2048
LOG2E = 1.4426950408889634
VMEM_LIMIT = 56 * 1024 * 1024


def _cparams(sem):
    return pltpu.CompilerParams(dimension_semantics=sem, vmem_limit_bytes=VMEM_LIMIT)


def _sigmoid(x):
    return 1.0 / (1.0 + jnp.exp(-x))


def _dot(a, b):
    return jnp.dot(a, b, preferred_element_type=F32)


def _dot_nt(a, b):
    return lax.dot_general(a, b, (((1,), (1,)), ((), ())), preferred_element_type=F32)


def _dot_tn(a, b):
    return lax.dot_general(a, b, (((0,), (0,)), ((), ())), preferred_element_type=F32)


def _seg_sum(x, ones_bd):
    return _dot(x.astype(BF16), ones_bd)


def _is_ctx(i):
    return jnp.where(i == 0, 1, 0)


def _mod_kernel(c_ref, w_ref, b_ref, o_ref):
    c = c_ref[...]
    s = (c * _sigmoid(c)).astype(BF16)
    o_ref[0] = _dot(s, w_ref[0]) + b_ref[0]


def _modulation(c8, ada_w, ada_b):
    L = ada_w.shape[0]
    tn = 1536
    return pl.pallas_call(
        _mod_kernel,
        grid=(L, 6 * D // tn),
        in_specs=[
            pl.BlockSpec((8, D), lambda l, j: (0, 0)),
            pl.BlockSpec((1, D, tn), lambda l, j: (l, 0, j)),
            pl.BlockSpec((1, 1, tn), lambda l, j: (l, 0, j)),
        ],
        out_specs=pl.BlockSpec((1, 8, tn), lambda l, j: (l, 0, j)),
        out_shape=jax.ShapeDtypeStruct((L, 8, 6 * D), F32),
        compiler_params=_cparams(("parallel", "parallel")),
        name="adaln_mod",
    )(c8, ada_w.astype(BF16), ada_b[:, None, :])


def _norm_mod(x, g, shift, scale):
    ms = jnp.mean(x * x, axis=-1, keepdims=True)
    return (x * lax.rsqrt(ms + NORM_EPS)) * g * (1.0 + scale) + shift


N_ROPE = 2 * DIFF_QK
IN_SPLITS = (RWKV_COLS, MLA_PAD, 3 * D, DIFF_COLS)
HALO = 8


def _inproj_kernel(x_ref, g_ref, mod_ref, w_ref, cos_ref, sin_ref, za_o, zc_o, zg_o, zb_o, first_o, last_o):
    h = _norm_mod(x_ref[0], g_ref[...], mod_ref[0, 0, 0:1, :], mod_ref[0, 0, 1:2, :]).astype(BF16)
    ofs = [0]
    for n in IN_SPLITS:
        ofs.append(ofs[-1] + n)
    proj = lambda j: _dot(h, w_ref[:, ofs[j]:ofs[j + 1]])
    za = proj(0)
    za_o[0] = za
    first_o[0, 0] = za[0:HALO]
    last_o[0, 0] = za[TM - HALO:TM]
    zc_o[0] = proj(1)
    zg_o[0] = proj(2)
    z = proj(3)
    cos = cos_ref[...]
    sin = sin_ref[...]
    half = DIFF_HD // 2
    first_half = jnp.bitwise_and(lax.broadcasted_iota(jnp.int32, (TM, LANE), 1), DIFF_HD - 1) < half
    for j in range(N_ROPE // LANE):
        sl = slice(j * LANE, (j + 1) * LANE)
        zj = z[:, sl]
        rot = jnp.where(first_half, -pltpu.roll(zj, LANE - half, 1), pltpu.roll(zj, half, 1))
        zb_o[0, :, sl] = zj * cos + rot * sin
    zb_o[0, :, N_ROPE:] = z[:, N_ROPE:]


def _inproj(x_all, g, mod, w, cos, sin):
    B, T, _ = x_all.shape
    nt = T // TM
    row = lambda n: pl.BlockSpec((1, TM, n), lambda b, i: (b, i, 0))
    edge = pl.BlockSpec((1, 1, HALO, RWKV_COLS), lambda b, i: (b, i, 0, 0))
    widths = (RWKV_COLS, MLA_PAD, 3 * D, DIFF_COLS)
    return pl.pallas_call(
        _inproj_kernel,
        grid=(B, nt),
        in_specs=[
            row(D),
            pl.BlockSpec((1, D), lambda b, i: (0, 0)),
            pl.BlockSpec((1, 1, 2, D), lambda b, i: (b, _is_ctx(i), 0, 0)),
            pl.BlockSpec(w.shape, lambda b, i: (0, 0), pipeline_mode=pl.Buffered(1)),
            pl.BlockSpec((TM, LANE), lambda b, i: (i, 0)),
            pl.BlockSpec((TM, LANE), lambda b, i: (i, 0)),
        ],
        out_specs=[row(n) for n in widths] + [edge, edge],
        out_shape=[jax.ShapeDtypeStruct((B, T, n), F32) for n in widths]
        + [jax.ShapeDtypeStruct((B, nt, HALO, RWKV_COLS), F32)] * 2,
        compiler_params=_cparams(("parallel", "parallel")),
        name="inproj",
    )(x_all, g, mod, w, cos, sin)


RWKV_PARAMS = ("mu", "w0", "w2", "a0", "a2", "g2", "k_k", "k_a", "r_k", "ones_bd")


SCAN_OPERANDS = ("r", "lw", "k", "v", "a", "b")


def _rwkv_tile(z, halo_prev, halo_next, prm, d):
    row = lax.broadcasted_iota(jnp.int32, z.shape, 0)
    prev = jnp.where(row == 0, halo_prev, pltpu.roll(z, 1, 0))
    nxt = jnp.where(row == TM - 1, halo_next, pltpu.roll(z, TM - 1, 0))
    zs = z + prm["mu"][0:1, :] * (prev - z) + prm["mu"][1:2, :] * (nxt - z)
    r = zs[:, 0:RW]
    k = zs[:, RW:2 * RW]
    v = zs[:, 2 * RW:3 * RW]
    wd = zs[:, 3 * RW:3 * RW + LANE]
    ad = zs[:, 3 * RW + LANE:3 * RW + 2 * LANE]
    gd = zs[:, 3 * RW + 2 * LANE:3 * RW + 3 * LANE]
    ones_bd = prm["ones_bd"][...]
    kk = k * prm["k_k"][...]
    sumsq = _seg_sum(kk * kk, ones_bd)
    u = _dot(jnp.tanh(wd).astype(BF16), prm["w2"][d])
    alr = _dot(ad.astype(BF16), prm["a2"][d])
    kk = kk / jnp.maximum(jnp.sqrt(sumsq), 1e-12)
    alr = _sigmoid(prm["a0"][d:d + 1, :] + alr)
    kd = k * (1.0 + (alr - 1.0) * prm["k_a"][...])
    return dict(
        r=r, v=v, a=-kk, k=kd, b=kk * alr, gd=gd,
        lw=-math.exp(-0.5) * _sigmoid(prm["w0"][d:d + 1, :] + u),
        bonus=_seg_sum(r * prm["r_k"][...] * kd, ones_bd) * v,
    )


def _wkv_consts(rev):
    t = lax.broadcasted_iota(jnp.int32, (CHUNK, LANE), 0)
    lane = lax.broadcasted_iota(jnp.int32, (CHUNK, LANE), 1)
    s = jnp.bitwise_and(lane, CHUNK - 1)
    tt = lax.broadcasted_iota(jnp.int32, (CHUNK, CHUNK), 0)
    ss = lax.broadcasted_iota(jnp.int32, (CHUNK, CHUNK), 1)
    r2 = lax.broadcasted_iota(jnp.int32, (LANE, LANE), 0)
    l2 = lax.broadcasted_iota(jnp.int32, (LANE, LANE), 1)
    return dict(
        strict=(s > t) if rev else (s < t),
        incl=(s >= t) if rev else (s <= t),
        eye=jnp.where(s == t, 1.0, 0.0).astype(F32),
        m0=lane < CHUNK,
        tri=jnp.where((ss >= tt) if rev else (ss <= tt), 1.0, 0.0).astype(F32),
        bdm=(r2 >= CHUNK) == (l2 >= CHUNK),
        ones=jnp.ones((LANE, LANE), F32),
        levels=[((t >> (j + 1)) == (s >> (j + 1))) & ((t >> j) != (s >> j)) for j in range(int(math.log2(CHUNK)))],
    )


def _bd(y, m0):
    z = jnp.zeros_like(y)
    return jnp.concatenate([jnp.where(m0, y, z), jnp.where(m0, z, y)], axis=0)


def _wkv_prepare_chunks(chunks):
    C = CHUNK
    zero = jnp.zeros((C, LANE), F32)
    n = range(len(chunks))
    cs = [ch[6] for ch in chunks]
    m0 = [c["m0"] for c in cs]
    lw = [ch[1] for ch in chunks]
    v = [ch[3] for ch in chunks]
    lw_hi = [x.astype(BF16).astype(F32) for x in lw]
    lw_lo = [lw[i] - lw_hi[i] for i in n]
    cc = [_dot(cs[i]["tri"], jnp.concatenate([lw_hi[i], lw_lo[i]], axis=1)) for i in n]
    tcol = [_dot_tn(jnp.concatenate([lw_hi[i], lw_lo[i]], axis=0), cs[i]["ones"]) for i in n]
    cum =[x[:, :LANE] + x[:, LANE:] for x in cc]
    total = [cum[i][0:1] if chunks[i][7] else cum[i][C - 1:C] for i in n]
    p_inc = [jnp.exp(x) for x in cum]
    p_prev = [jnp.exp(cum[i] - lw[i]) for i in n]
    p_inv = [jnp.exp(-x) for x in cum]
    p_sc = [jnp.exp(total[i] - cum[i]) for i in n]
    rt = [chunks[i][0] * p_inc[i] for i in n]
    at = [chunks[i][4] * p_prev[i] for i in n]
    bt = [chunks[i][5] * p_inv[i] for i in n]
    kt = [chunks[i][2] * p_inv[i] for i in n]
    bks = [jnp.concatenate([chunks[i][5] * p_sc[i], chunks[i][2] * p_sc[i]], axis=0) for i in n]
    pw = [_dot_nt(jnp.concatenate([at[i], rt[i]], axis=0),
                  jnp.concatenate([_bd(bt[i], m0[i]), _bd(kt[i], m0[i])], axis=0)) for i in n]
    ab = [jnp.where(cs[i]["strict"], pw[i][:C, :LANE], zero) for i in n]
    akrk = [jnp.concatenate([jnp.where(cs[i]["strict"], pw[i][:C, LANE:], zero),
                             jnp.where(cs[i]["incl"], pw[i][C:, LANE:], zero)], axis=0) for i in n]
    rb = [jnp.where(cs[i]["incl"], pw[i][C:, :LANE], zero) for i in n]
    tinv = [cs[i]["eye"] + jnp.where(cs[i]["levels"][0], ab[i], zero) for i in n]
    for j in range(1, len(cs[0]["levels"])):
        tl = [_dot(tinv[i], _bd(jnp.where(cs[i]["levels"][j], ab[i], zero), m0[i])) for i in n]
        tinv = [tinv[i] + _dot(tl[i], _bd(tinv[i], m0[i])) for i in n]
    akv = [_dot(akrk[i], _bd(v[i], m0[i])) for i in n]
    wx = [_dot(tinv[i], jnp.concatenate([_bd(at[i], m0[i]), _bd(akv[i][:C], m0[i])], axis=1)) for i in n]
    return [dict(wr=jnp.concatenate([wx[i][:, :LANE], rt[i]], axis=0), x=wx[i][:, LANE:], yk=akv[i][C:], rb=rb[i],
                 bks=bks[i], v=v[i], pcol=jnp.exp(tcol[i])) for i in n]


WKV_PAIRS = RW // LANE


def _wkv_kernel(zf_ref, zb_ref, hpf_ref, hnf_ref, hpb_ref, hnb_ref, *rest):
    prm = dict(zip(RWKV_PARAMS, rest))
    yf_o, yb_o, bonf_o, bonb_o, g_o, h_s = rest[len(RWKV_PARAMS):]

    @pl.when(pl.program_id(1) == 0)
    def _():
        h_s[...] = jnp.zeros_like(h_s)

    C = CHUNK
    nch = TM // C
    tiles = (_rwkv_tile(zf_ref[0], hpf_ref[0, 0], hnf_ref[0, 0], prm, 0),
             _rwkv_tile(zb_ref[0], hpb_ref[0, 0], hnb_ref[0, 0], prm, 1))
    bonf_o[0] = tiles[0]["bonus"]
    bonb_o[0] = tiles[1]["bonus"]
    g_o[0] = _dot(_sigmoid(tiles[0]["gd"]).astype(BF16), prm["g2"][...])
    dirs = ((False, tiles[0], yf_o), (True, tiles[1], yb_o))
    order = {False: list(range(nch)), True: list(range(nch - 1, -1, -1))}
    chains = [(rev, tile, y_o, slice(p * LANE, (p + 1) * LANE)) for rev, tile, y_o in dirs for p in range(WKV_PAIRS)]
    consts = {rev: _wkv_consts(rev) for rev, _, _ in dirs}
    chunks = []
    for rev, tile, _, lanes in chains:
        for c in order[rev]:
            chunks.append(tuple(tile[name][c * C:(c + 1) * C, lanes] for name in SCAN_OPERANDS) + (consts[rev], rev))
    prepared = _wkv_prepare_chunks(chunks)
    m0 = consts[False]["m0"]
    bdm = consts[False]["bdm"]
    nc = range(len(chains))
    states = [h_s[j] for j in nc]
    for step in range(nch):
        ps = [prepared[j * nch + step] for j in nc]
        s1 = [_dot(ps[j]["wr"], states[j]) for j in nc]
        u = [s1[j][:C] + ps[j]["x"] for j in nc]
        s2 = [_dot(ps[j]["rb"], _bd(u[j], m0)) for j in nc]
        s3 = [_dot_tn(ps[j]["bks"], jnp.concatenate([u[j], ps[j]["v"]], axis=0)) for j in nc]
        for j, (rev, _, y_o, lanes) in enumerate(chains):
            c = order[rev][step]
            y_o[0, c * C:(c + 1) * C, lanes] = s1[j][C:] + s2[j] + ps[j]["yk"]
        states = [ps[j]["pcol"] * states[j] + jnp.where(bdm, s3[j], jnp.zeros_like(s3[j])) for j in nc]
    for j in nc:
        h_s[j] = states[j]


def _wkv_scan(za, hprev, hnext, p):
    B, T, _ = za.shape
    nt = T // TM
    rev_tile = lambda j: jnp.where(j == 0, 0, nt - j)
    tile = lambda n, idx: pl.BlockSpec((1, TM, n), lambda b, i: (b, idx(i), 0))
    halo = lambda idx: pl.BlockSpec((1, 1, 1, RWKV_COLS), lambda b, i: (b, idx(i), 0, 0))
    full = lambda a: pl.BlockSpec(a.shape, lambda b, i: (0,) * a.ndim)
    fwd = lambda i: i
    params = [p[name] for name in RWKV_PARAMS]
    return pl.pallas_call(
        _wkv_kernel,
        grid=(B, nt),
        in_specs=[tile(RWKV_COLS, fwd), tile(RWKV_COLS, rev_tile), halo(fwd), halo(fwd), halo(rev_tile),
                  halo(rev_tile)] + [full(a) for a in params],
        out_specs=[tile(RW, fwd), tile(RW, rev_tile), tile(RW, fwd), tile(RW, rev_tile), tile(RW, fwd)],
        out_shape=[jax.ShapeDtypeStruct((B, T, RW), F32)] * 5,
        scratch_shapes=[pltpu.VMEM((2 * WKV_PAIRS, LANE, LANE), F32)],
        compiler_params=_cparams(("parallel", "arbitrary")),
        name="wkv_scan",
    )(za, za, hprev, hnext, hprev, hnext, *params)


def _load_vt(v_ref, vt_s):
    for c in range(v_ref.shape[1] // TK):
        vt_s[:, c * TK:(c + 1) * TK] = v_ref[0, c * TK:(c + 1) * TK, :].T.astype(BF16)


def _key_chunks(n_keys, ctx_only):
    chunks = [(0, CTX)]
    if not ctx_only:
        big = min(KEY_BLOCK, n_keys - CTX)
        chunks += [(CTX + j * big, big) for j in range((n_keys - CTX) // big)]
    return chunks


def _attend(streams, k_ref, vt_s, chunks):
    state = [(jnp.full((1, TM), -1e30, F32), jnp.zeros((1, TM), F32),
              jnp.zeros((rows.stop - rows.start, TM), F32)) for _, _, rows in streams]
    subs = [[(start + t, min(TK, size - t)) for t in range(0, size, TK)] for start, size in chunks]

    ns = range(len(streams))

    def score(c, t, j):
        st, sz = subs[c][t]
        q, lanes, _ = streams[j]
        s = _dot_nt(k_ref[0, st:st + sz, lanes], q)
        return s, jnp.max(s, axis=0, keepdims=True)

    cur = [[score(0, t, j) for j in ns] for t in range(len(subs[0]))]
    for c in range(len(chunks)):
        n_next = len(subs[c + 1]) if c + 1 < len(chunks) else 0
        nxt = [[] for _ in range(n_next)]
        new_state = []
        for j in ns:
            m, l, acc = state[j]
            m_new = functools.reduce(jnp.maximum, [row[j][1] for row in cur], m)
            alpha = jnp.exp2(m - m_new)
            new_state.append((m_new, alpha * l, alpha * acc))
        for t, (st, sz) in enumerate(subs[c]):
            for j, (_, _, rows) in enumerate(streams):
                if t < n_next:
                    nxt[t].append(score(c + 1, t, j))
                m_new, l, acc = new_state[j]
                p = jnp.exp2(cur[t][j][0] - m_new)
                new_state[j] = (m_new, l + jnp.sum(p, axis=0, keepdims=True),
                                acc + _dot(vt_s[rows, st:st + sz], p.astype(BF16)))
        for t in range(len(subs[c]), n_next):
            nxt[t] = [score(c + 1, t, j) for j in ns]
        state = new_state
        cur = nxt
    return [(acc, l) for _, l, acc in state]


LAG_BLOCK = 4096
SAFE_EXCESS = 64.0


def _attend_lagged(streams, k_ref, vt_s):
    n_keys = k_ref.shape[1]
    ns = range(len(streams))
    m, l, acc = [], [], []
    for q, lanes, rows in streams:
        s = _dot_nt(k_ref[0, 0:CTX, lanes], q)
        mj = jnp.max(s, axis=0, keepdims=True)
        p = jnp.exp2(s - mj)
        m.append(mj)
        l.append(jnp.sum(p, axis=0, keepdims=True))
        acc.append(_dot(vt_s[rows, 0:CTX], p.astype(BF16)))
    score = lambda st, sz, j: _dot_nt(k_ref[0, st:st + sz, streams[j][1]], streams[j][0])
    excess = jnp.zeros((1, TM), F32)
    big = min(LAG_BLOCK, n_keys - CTX)
    for start in range(CTX, n_keys, big):
        subs = [(start + t, min(TK, big - t)) for t in range(0, big, TK)]
        top = list(m)
        cur = [score(*subs[0], j) for j in ns]
        for t, (st, sz) in enumerate(subs):
            nxt = []
            for j in ns:
                if t + 1 < len(subs):
                    nxt.append(score(*subs[t + 1], j))
                s = cur[j]
                top[j] = jnp.maximum(top[j], jnp.max(s, axis=0, keepdims=True))
                p = jnp.exp2(s - m[j])
                l[j] = l[j] + jnp.sum(p, axis=0, keepdims=True)
                acc[j] = acc[j] + _dot(vt_s[streams[j][2], st:st + sz], p.astype(BF16))
            cur = nxt
        for j in ns:
            excess = jnp.maximum(excess, top[j] - m[j])
            alpha = jnp.exp2(m[j] - top[j])
            m[j], l[j], acc[j] = top[j], alpha * l[j], alpha * acc[j]
    return [(acc[j], l[j]) for j in ns], excess


def _run_attention(i, streams, k_ref, vt_s, finish):
    n_keys = k_ref.shape[1]

    @pl.when(i == 0)
    def _():
        finish(_attend(streams, k_ref, vt_s, _key_chunks(n_keys, True)))

    @pl.when(i > 0)
    def _():
        res, excess = _attend_lagged(streams, k_ref, vt_s)
        finish(res)

        @pl.when(jnp.max(excess) > SAFE_EXCESS)
        def _():
            finish(_attend(streams, k_ref, vt_s, _key_chunks(n_keys, False)))


DIFF_HEADS = 2


def _diff_attn_kernel(lam_init, q_ref, k_ref, v_ref, lamv_ref, g_ref, o_ref, vt_s):
    i = pl.program_id(2)

    @pl.when(i == 0)
    def _():
        _load_vt(v_ref, vt_s)

    lv = lamv_ref[...]
    lam = (jnp.exp(jnp.sum(lv[0:1] * lv[1:2], axis=-1, keepdims=True))
           - jnp.exp(jnp.sum(lv[2:3] * lv[3:4], axis=-1, keepdims=True)) + lam_init)
    lane = lax.broadcasted_iota(jnp.int32, (TM, LANE), 1)
    zq = jnp.zeros((TM, LANE), F32)
    heads = [slice(h * LANE, (h + 1) * LANE) for h in range(DIFF_HEADS)]
    streams = []
    for lanes in heads:
        q = q_ref[0, :, lanes] * (DIFF_HD ** -0.5 * LOG2E)
        streams += [(jnp.where(lane < DIFF_HD, q, zq), lanes, lanes), (jnp.where(lane < DIFF_HD, zq, q), lanes, lanes)]

    def finish(res):
        for h, lanes in enumerate(heads):
            (a1, l1), (a2, l2) = res[2 * h], res[2 * h + 1]
            o = a1 / l1 - lam * (a2 / l2)
            ms = jnp.mean(o * o, axis=0, keepdims=True)
            o = o * lax.rsqrt(ms + NORM_EPS)
            o_ref[0, :, lanes] = o.T * g_ref[...] * (1.0 - lam_init)

    _run_attention(i, streams, k_ref, vt_s, finish)


def _diff_attention(zb, lam_vecs, subln_g, layer_idx):
    B, T, _ = zb.shape
    nt = T // TM
    lam_init = 0.8 - 0.6 * math.exp(-0.3 * layer_idx)
    width = DIFF_HEADS * LANE
    groups = H_B // DIFF_HEADS
    return pl.pallas_call(
        functools.partial(_diff_attn_kernel, lam_init),
        grid=(B, groups, nt),
        in_specs=[
            pl.BlockSpec((1, TM, width), lambda b, h, i: (b, i, h)),
            pl.BlockSpec((1, T, width), lambda b, h, i: (b, 0, groups + h)),
            pl.BlockSpec((1, T, width), lambda b, h, i: (b, 0, 2 * groups + h)),
            pl.BlockSpec((4, DIFF_HD), lambda b, h, i: (0, 0)),
            pl.BlockSpec((1, LANE), lambda b, h, i: (0, 0)),
        ],
        out_specs=pl.BlockSpec((1, TM, width), lambda b, h, i: (b, i, h)),
        out_shape=jax.ShapeDtypeStruct((B, T, H_B * LANE), F32),
        scratch_shapes=[pltpu.VMEM((width, T), BF16)],
        compiler_params=_cparams(("parallel", "parallel", "arbitrary")),
        name="diff_attn",
    )(zb, zb, zb, lam_vecs, subln_g)


def _mla_proj_kernel(z_ref, qg_ref, kvg_ref, wq_ref, wqr_ref, wk_ref, wv_ref, cq_ref, sq_ref, ck_ref, sk_ref,
                     q_o, k_o, v_o):
    z = z_ref[0]
    cq = z[:, 0:Q_LORA]
    ckv = z[:, Q_LORA:Q_LORA + KV_LORA]
    kr = z[:, 5 * LANE:6 * LANE] * ck_ref[...] + z[:, 6 * LANE:7 * LANE] * sk_ref[...]
    rms = lambda t: t * lax.rsqrt(jnp.mean(t * t, axis=-1, keepdims=True) + NORM_EPS)
    cqn = (rms(cq) * qg_ref[...]).astype(BF16)
    ckvn = rms(ckv) * kvg_ref[...]
    q = _dot(cqn, wq_ref[...])
    qr = _dot(cqn, wqr_ref[...])
    cos = cq_ref[...]
    sin = sq_ref[...]
    scale = (MLA_NOPE + MLA_ROPE) ** -0.5 * LOG2E
    for h in range(H_C):
        sl = slice(h * LANE, (h + 1) * LANE)
        q_o[0, :, sl] = (q[:, sl] * cos + qr[:, sl] * sin) * scale
    kin = jnp.concatenate([ckvn, kr], axis=1).astype(BF16)
    k_o[0] = _dot(kin, wk_ref[...])
    v_o[0] = _dot(ckvn.astype(BF16), wv_ref[...])


def _mla_project(zc, p):
    B, T, _ = zc.shape
    full = lambda a: pl.BlockSpec(a.shape, lambda b, i: (0,) * a.ndim)
    tab = pl.BlockSpec((TM, LANE), lambda b, i: (i, 0))
    ws = [p["q_g"], p["kv_g"], p["w_uq"], p["w_uq_rot"], p["w_uk"], p["w_uv"]]
    return pl.pallas_call(
        _mla_proj_kernel,
        grid=(B, T // TM),
        in_specs=[pl.BlockSpec((1, TM, MLA_PAD), lambda b, i: (b, i, 0))] + [full(a) for a in ws] + [tab] * 4,
        out_specs=[
            pl.BlockSpec((1, TM, H_C * LANE), lambda b, i: (b, i, 0)),
            pl.BlockSpec((1, TM, H_C * LANE), lambda b, i: (b, i, 0)),
            pl.BlockSpec((1, TM, H_C * MLA_V), lambda b, i: (b, i, 0)),
        ],
        out_shape=[
            jax.ShapeDtypeStruct((B, T, H_C * LANE), F32),
            jax.ShapeDtypeStruct((B, T, H_C * LANE), F32),
            jax.ShapeDtypeStruct((B, T, H_C * MLA_V), F32),
        ],
        compiler_params=_cparams(("parallel", "parallel")),
        name="mla_project",
    )(zc, *ws, p["cos_q"], p["sin_q"], p["cos_k"], p["sin_k"])


MLA_HEADS = 4


def _mla_attn_kernel(q_ref, k_ref, v_ref, o_ref, vt_s):
    i = pl.program_id(2)

    @pl.when(i == 0)
    def _():
        _load_vt(v_ref, vt_s)

    streams = [(q_ref[0, :, h * LANE:(h + 1) * LANE], slice(h * LANE, (h + 1) * LANE),
                slice(h * MLA_V, (h + 1) * MLA_V)) for h in range(MLA_HEADS)]

    def finish(res):
        for p in range(MLA_HEADS // 2):
            (a0, l0), (a1, l1) = res[2 * p], res[2 * p + 1]
            o_ref[0, :, p * LANE:(p + 1) * LANE] = jnp.concatenate([a0 / l0, a1 / l1], axis=0).T

    _run_attention(i, streams, k_ref, vt_s, finish)


def _mla_attention(q, k, v):
    B, T, _ = q.shape
    nt = T // TM
    return pl.pallas_call(
        _mla_attn_kernel,
        grid=(B, H_C // MLA_HEADS, nt),
        in_specs=[
            pl.BlockSpec((1, TM, MLA_HEADS * LANE), lambda b, p, i: (b, i, p)),
            pl.BlockSpec((1, T, MLA_HEADS * LANE), lambda b, p, i: (b, 0, p)),
            pl.BlockSpec((1, T, MLA_HEADS * MLA_V), lambda b, p, i: (b, 0, p)),
        ],
        out_specs=pl.BlockSpec((1, TM, MLA_HEADS * MLA_V), lambda b, p, i: (b, i, p)),
        out_shape=jax.ShapeDtypeStruct((B, T, H_C * MLA_V), F32),
        scratch_shapes=[pltpu.VMEM((MLA_HEADS * MLA_V, T), BF16)],
        compiler_params=_cparams(("parallel", "parallel", "arbitrary")),
        name="mla_attn",
    )(q, k, v)


def _merge_kernel(x_ref, yf_ref, yb_ref, bonf_ref, bonb_ref, g_ref, yd_ref, yc_ref, zg_ref, g1_ref, lnw_ref,
                  lnb_ref, ones_ref, wb_ref, wo_ref, o_ref):
    ones_bd = ones_ref[...]
    y = yf_ref[0] + yb_ref[0]
    mean = _seg_sum(y, ones_bd) * (1.0 / N_A)
    yc = y - mean
    var = _seg_sum(yc * yc, ones_bd) * (1.0 / N_A)
    bonus = bonf_ref[0] + bonb_ref[0]
    ya = (yc * lax.rsqrt(var + GN_EPS) * lnw_ref[...] + lnb_ref[...] + bonus) * g_ref[0]
    acc = None
    for j, yj in enumerate((ya, yd_ref[0], yc_ref[0])):
        gate = _sigmoid(zg_ref[0, :, j * D:(j + 1) * D])
        t = gate * _dot(yj.astype(BF16), wb_ref[j])
        acc = t if acc is None else acc + t
    o_ref[0] = x_ref[0] + g1_ref[0, 0] * _dot(acc.astype(BF16), wo_ref[...])


def _merge(x_all, yf, yb, bonf, bonb, g, yd, yc, zg, g1, lnw, lnb, ones_bd, wb, wo, latent_only):
    B, T, _ = x_all.shape
    skip = 1 if latent_only else 0
    row = lambda n: pl.BlockSpec((1, TM, n), lambda b, i: (b, i + skip, 0))
    full = lambda a: pl.BlockSpec(a.shape, lambda b, i: (0,) * a.ndim)
    return pl.pallas_call(
        _merge_kernel,
        grid=(B, T // TM - skip),
        in_specs=[row(D)] + [row(RW)] * 7 + [row(3 * D),
                  pl.BlockSpec((1, 1, 1, D), lambda b, i: (b, _is_ctx(i + skip), 0, 0)),
                  full(lnw), full(lnb), full(ones_bd), full(wb), full(wo)],
        out_specs=pl.BlockSpec((1, TM, D), lambda b, i: (b, i, 0)),
        out_shape=jax.ShapeDtypeStruct((B, T - skip * TM, D), F32),
        compiler_params=_cparams(("parallel", "parallel")),
        name="merge",
    )(x_all, yf, yb, bonf, bonb, g, yd, yc, zg, g1, lnw, lnb, ones_bd, wb, wo)


def _ffn_kernel(final, x_ref, g_ref, mod_ref, wg_ref, wu_ref, wd_ref, fg_ref, o_ref):
    x = x_ref[0]
    h = _norm_mod(x, g_ref[...], mod_ref[0, 0, 0:1, :], mod_ref[0, 0, 1:2, :]).astype(BF16)
    gate = _dot(h, wg_ref[...])
    up = _dot(h, wu_ref[...])
    act = (gate * _sigmoid(gate) * up).astype(BF16)
    out = x + mod_ref[0, 0, 2:3, :] * _dot(act, wd_ref[...])
    if final:
        ms = jnp.mean(out * out, axis=-1, keepdims=True)
        out = out * lax.rsqrt(ms + NORM_EPS) * fg_ref[...]
    o_ref[0] = out


def _ffn(x_all, g, mod, wg, wu, wd, fg, final):
    B, T, _ = x_all.shape
    full = lambda a: pl.BlockSpec(a.shape, lambda b, i: (0,) * a.ndim, pipeline_mode=pl.Buffered(1))
    return pl.pallas_call(
        functools.partial(_ffn_kernel, final),
        grid=(B, T // TM),
        in_specs=[
            pl.BlockSpec((1, TM, D), lambda b, i: (b, i, 0)),
            pl.BlockSpec((1, D), lambda b, i: (0, 0)),
            pl.BlockSpec((1, 1, 3, D), lambda b, i: (b, 0 if final else _is_ctx(i), 0, 0)),
            full(wg), full(wu), full(wd),
            pl.BlockSpec((1, D), lambda b, i: (0, 0)),
        ],
        out_specs=pl.BlockSpec((1, TM, D), lambda b, i: (b, i, 0)),
        out_shape=jax.ShapeDtypeStruct((B, T, D), F32),
        compiler_params=_cparams(("parallel", "parallel")),
        name="ffn",
    )(x_all, g, mod, wg, wu, wd, fg)


def _rot_cols(w, half):
    d, n = w.shape
    w4 = w.reshape(d, n // (2 * half), 2, half)
    return jnp.concatenate([-w4[:, :, 1:2], w4[:, :, 0:1]], axis=2).reshape(d, n)


def _rope_tables(n_lat):
    rows = n_lat // GRID_W
    row = jnp.repeat(jnp.arange(rows, dtype=F32), GRID_W)
    col = jnp.tile(jnp.arange(GRID_W, dtype=F32), rows)

    def tables(rot_dim):
        n_freq = rot_dim // 4
        inv_freq = ROPE_BASE ** (-jnp.arange(n_freq, dtype=F32) / n_freq)
        ang = jnp.concatenate([row[:, None] * inv_freq, col[:, None] * inv_freq], axis=-1)
        cos = jnp.concatenate([jnp.ones((CTX, rot_dim // 2), F32), jnp.cos(ang)], axis=0)
        sin = jnp.concatenate([jnp.zeros((CTX, rot_dim // 2), F32), jnp.sin(ang)], axis=0)
        return cos, sin

    T = CTX + n_lat
    cos_b, sin_b = tables(DIFF_HD)
    cos_c, sin_c = tables(MLA_ROPE)
    one = lambda n: jnp.ones((T, n), F32)
    zero = lambda n: jnp.zeros((T, n), F32)
    return dict(
        cos_b=jnp.tile(cos_b, (1, 4)), sin_b=jnp.tile(sin_b, (1, 4)),
        cos_q=jnp.concatenate([one(MLA_NOPE), cos_c, cos_c, one(32)], axis=1),
        sin_q=jnp.concatenate([zero(MLA_NOPE), sin_c, sin_c, zero(32)], axis=1),
        cos_k=jnp.concatenate([cos_c, cos_c, one(96)], axis=1),
        sin_k=jnp.concatenate([sin_c, sin_c, zero(96)], axis=1),
    )


def _pad_heads(w, per_head, n_heads):
    d = w.shape[0]
    w3 = w.reshape(d, n_heads, per_head)
    return jnp.pad(w3, ((0, 0), (0, 0), (0, LANE - per_head))).reshape(d, n_heads * LANE)


def _layer_weights(l, ws):
    (w_in, mu, w0, w2, a0, a2, g2, k_k, k_a, r_k, ln_w, ln_b, q_g, kv_g, w_uq, w_uk, w_uv, w_branch, w_out,
     wg, wu, wd) = [w[l] for w in ws]
    o1, o2, o3 = RWKV_COLS, RWKV_COLS + DIFF_COLS, RWKV_COLS + DIFF_COLS + MLA_COLS
    w_a, w_b, w_c, w_g = w_in[:, :o1], w_in[:, o1:o2], w_in[:, o2:o3], w_in[:, o3:]
    w_kr = w_c[:, Q_LORA + KV_LORA:]
    w_c_ext = jnp.concatenate([w_c, jnp.zeros((D, 96), F32), _rot_cols(w_kr, MLA_ROPE // 2),
                               jnp.zeros((D, 96), F32)], axis=1)
    zpad = jnp.zeros((2, 64, RW), F32)
    seg = jnp.arange(RW) // N_A
    w_uq_p = _pad_heads(w_uq, MLA_NOPE + MLA_ROPE, H_C)
    w_uq_rope = w_uq.reshape(Q_LORA, H_C, MLA_NOPE + MLA_ROPE)[:, :, MLA_NOPE:].reshape(Q_LORA, H_C * MLA_ROPE)
    w_uq_rot = _rot_cols(w_uq_rope, MLA_ROPE // 2).reshape(Q_LORA, H_C, MLA_ROPE)
    w_uq_rot_p = jnp.pad(w_uq_rot, ((0, 0), (0, 0), (MLA_NOPE, LANE - MLA_NOPE - MLA_ROPE))).reshape(Q_LORA, H_C * LANE)
    place = jnp.zeros((LANE, H_C, LANE), F32).at[
        jnp.arange(MLA_ROPE)[:, None], jnp.arange(H_C)[None, :], MLA_NOPE + jnp.arange(MLA_ROPE)[:, None]].set(1.0)
    w_uk_p = jnp.concatenate([_pad_heads(w_uk, MLA_NOPE, H_C), place.reshape(LANE, H_C * LANE)], axis=0)
    return dict(
        w_in=jnp.concatenate([w_a, w_c_ext, w_g, w_b], axis=1).astype(BF16),
        rwkv=dict(
            mu=mu, w0=w0, a0=a0,
            w2=jnp.stack([jnp.concatenate([w2[0], zpad[0]], 0), jnp.concatenate([zpad[1], w2[1]], 0)]).astype(BF16),
            a2=jnp.stack([jnp.concatenate([a2[0], zpad[0]], 0), jnp.concatenate([zpad[1], a2[1]], 0)]).astype(BF16),
            g2=g2.astype(BF16), k_k=k_k[None], k_a=k_a[None], r_k=r_k.reshape(1, RW),
            ones_bd=(seg[:, None] == seg[None, :]).astype(BF16)),
        ln_w=ln_w[None], ln_b=ln_b[None],
        mla=dict(q_g=q_g[None], kv_g=kv_g[None], w_uq=w_uq_p.astype(BF16), w_uq_rot=w_uq_rot_p.astype(BF16),
                 w_uk=w_uk_p.astype(BF16), w_uv=w_uv.astype(BF16)),
        w_branch=w_branch.astype(BF16), w_out=w_out.astype(BF16),
        wg=wg.astype(BF16), wu=wu.astype(BF16), wd=wd.astype(BF16),
    )


def _mod_rows(m, idx, B):
    lat = m[:B][:, idx]
    ctx = jnp.broadcast_to(m[B][idx][None], lat.shape)
    return jnp.stack([lat, ctx], axis=1)


def kernel(x, c, ctx, c_ctx, ada_w, ada_b, norm1_g, norm2_g, w_in, rwkv_shift_mu, rwkv_w0, rwkv_w2, rwkv_a0,
           rwkv_a2, rwkv_g2, rwkv_k_k, rwkv_k_a, rwkv_r_k, rwkv_ln_w, rwkv_ln_b, diff_lambda, diff_subln_g,
           mla_q_norm_g, mla_kv_norm_g, mla_w_uq, mla_w_uk, mla_w_uv, w_branch, w_out, ffn_w_gate, ffn_w_up,
           ffn_w_down, final_norm_g):
    B, S, _ = x.shape
    L = ada_w.shape[0]
    T = CTX + S
    nt = T // TM
    x_all = jnp.concatenate([ctx, x], axis=1)
    c8 = jnp.concatenate([c, c_ctx[None], jnp.zeros((8 - B - 1, D), F32)], axis=0)
    mod = _modulation(c8, ada_w, ada_b).reshape(L, 8, 6, D)
    tabs = _rope_tables(S)
    ws = (w_in, rwkv_shift_mu, rwkv_w0, rwkv_w2, rwkv_a0, rwkv_a2, rwkv_g2, rwkv_k_k, rwkv_k_a, rwkv_r_k,
          rwkv_ln_w, rwkv_ln_b, mla_q_norm_g, mla_kv_norm_g, mla_w_uq, mla_w_uk, mla_w_uv, w_branch, w_out,
          ffn_w_gate, ffn_w_up, ffn_w_down)
    for l in range(L):
        p = _layer_weights(l, ws)
        mod1 = _mod_rows(mod[l], jnp.array([0, 1]), B)
        g1 = _mod_rows(mod[l], jnp.array([2]), B)
        mod2 = _mod_rows(mod[l], jnp.array([3, 4, 5]), B)
        n1 = norm1_g[l][None]

        za, zc, zg, zb, first8, last8 = _inproj(x_all, n1, mod1, p["w_in"], tabs["cos_b"], tabs["sin_b"])

        zeros = lambda n: jnp.zeros((B, n, RWKV_COLS), F32)
        last_rows, first_rows = last8[:, :, HALO - 1], first8[:, :, 0]
        hprev = jnp.concatenate([zeros(2), last_rows[:, 1:nt - 1]], axis=1)[:, :, None]
        hnext = jnp.concatenate([zeros(1), first_rows[:, 2:nt], zeros(1)], axis=1)[:, :, None]
        yf, yb, bonf, bonb, g = _wkv_scan(za, hprev, hnext, p["rwkv"])

        yd = _diff_attention(zb, diff_lambda[l], diff_subln_g[l][None], l)

        q, k, vm = _mla_project(zc, dict(p["mla"], cos_q=tabs["cos_q"], sin_q=tabs["sin_q"],
                                         cos_k=tabs["cos_k"], sin_k=tabs["sin_k"]))
        yc = _mla_attention(q, k, vm)

        last = l == L - 1
        x_all = _merge(x_all, yf, yb, bonf, bonb, g, yd, yc, zg, g1, p["ln_w"], p["ln_b"], p["rwkv"]["ones_bd"],
                       p["w_branch"], p["w_out"], last)
        x_all = _ffn(x_all, norm2_g[l][None], mod2, p["wg"], p["wu"], p["wd"], final_norm_g[None], last)
    return x_all
```

```python
import functools
import math

import jax
import jax.numpy as jnp
from jax import lax
from jax.experimental import pallas as pl
from jax.experimental.pallas import tpu as pltpu

F32 = jnp.float32
BF16 = jnp.bfloat16

D = 1024
CTX = 256
GRID_W = 64
ROPE_BASE = 10000.0
NORM_EPS = 1e-6
GN_EPS = 64e-5

H_A, N_A = 8, 64
RW = H_A * N_A
RWKV_COLS = 3 * RW + 2 * 64 + 2 * 64 + 128
H_B, DIFF_HD = 4, 64
DIFF_QK = H_B * 2 * DIFF_HD
DIFF_COLS = 3 * DIFF_QK
H_C = 8
Q_LORA, KV_LORA = 384, 256
MLA_NOPE, MLA_ROPE, MLA_V = 64, 32, 64
MLA_COLS = Q_LORA + KV_LORA + MLA_ROPE
MLA_PAD = 896
D_FF = 2816

TM = 256
LANE = 128
CHUNK = 64
TK = 256
KEY_BLOCK = 2048
LOG2E = 1.4426950408889634
VMEM_LIMIT = 56 * 1024 * 1024


def _cparams(sem):
    return pltpu.CompilerParams(dimension_semantics=sem, vmem_limit_bytes=VMEM_LIMIT)


def _sigmoid(x):
    return 1.0 / (1.0 + jnp.exp(-x))


def _dot(a, b):
    return jnp.dot(a, b, preferred_element_type=F32)


def _dot_nt(a, b):
    return lax.dot_general(a, b, (((1,), (1,)), ((), ())), preferred_element_type=F32)


def _dot_tn(a, b):
    return lax.dot_general(a, b, (((0,), (0,)), ((), ())), preferred_element_type=F32)


def _seg_sum(x, ones_bd):
    return _dot(x.astype(BF16), ones_bd)


def _is_ctx(i):
    return jnp.where(i == 0, 1, 0)


def _mod_kernel(c_ref, w_ref, b_ref, o_ref):
    c = c_ref[...]
    s = (c * _sigmoid(c)).astype(BF16)
    o_ref[0] = _dot(s, w_ref[0].astype(BF16)) + b_ref[0]


def _modulation(c8, ada_w, ada_b):
    L = ada_w.shape[0]
    tn = 1536
    return pl.pallas_call(
        _mod_kernel,
        grid=(L, 6 * D // tn),
        in_specs=[
            pl.BlockSpec((8, D), lambda l, j: (0, 0)),
            pl.BlockSpec((1, D, tn), lambda l, j: (l, 0, j)),
            pl.BlockSpec((1, 1, tn), lambda l, j: (l, 0, j)),
        ],
        out_specs=pl.BlockSpec((1, 8, tn), lambda l, j: (l, 0, j)),
        out_shape=jax.ShapeDtypeStruct((L, 8, 6 * D), F32),
        compiler_params=_cparams(("parallel", "parallel")),
        name="adaln_mod",
    )(c8, ada_w, ada_b[:, None, :])


def _norm_mod(x, g, shift, scale):
    ms = jnp.mean(x * x, axis=-1, keepdims=True)
    return (x * lax.rsqrt(ms + NORM_EPS)) * g * (1.0 + scale) + shift


N_ROPE = 2 * DIFF_QK
IN_SPLITS = (RWKV_COLS, MLA_PAD, 3 * D, DIFF_COLS)
HALO = 8


def _inproj_kernel(x_ref, g_ref, mod_ref, w_ref, cos_ref, sin_ref, za_o, zc_o, zg_o, zb_o, first_o, last_o):
    h = _norm_mod(x_ref[0], g_ref[...], mod_ref[0, 0, 0:1, :], mod_ref[0, 0, 1:2, :]).astype(BF16)
    ofs = [0]
    for n in IN_SPLITS:
        ofs.append(ofs[-1] + n)
    proj = lambda j: _dot(h, w_ref[:, ofs[j]:ofs[j + 1]])
    za = proj(0)
    za_o[0] = za
    first_o[0, 0] = za[0:HALO]
    last_o[0, 0] = za[TM - HALO:TM]
    zc_o[0] = proj(1)
    zg_o[0] = proj(2)
    z = proj(3)
    cos = cos_ref[...]
    sin = sin_ref[...]
    half = DIFF_HD // 2
    first_half = jnp.bitwise_and(lax.broadcasted_iota(jnp.int32, (TM, LANE), 1), DIFF_HD - 1) < half
    for j in range(N_ROPE // LANE):
        sl = slice(j * LANE, (j + 1) * LANE)
        zj = z[:, sl]
        rot = jnp.where(first_half, -pltpu.roll(zj, LANE - half, 1), pltpu.roll(zj, half, 1))
        zb_o[0, :, sl] = zj * cos + rot * sin
    zb_o[0, :, N_ROPE:] = z[:, N_ROPE:]


def _inproj(x_all, g, mod, w, cos, sin):
    B, T, _ = x_all.shape
    nt = T // TM
    row = lambda n: pl.BlockSpec((1, TM, n), lambda b, i: (b, i, 0))
    edge = pl.BlockSpec((1, 1, HALO, RWKV_COLS), lambda b, i: (b, i, 0, 0))
    widths = (RWKV_COLS, MLA_PAD, 3 * D, DIFF_COLS)
    return pl.pallas_call(
        _inproj_kernel,
        grid=(B, nt),
        in_specs=[
            row(D),
            pl.BlockSpec((1, D), lambda b, i: (0, 0)),
            pl.BlockSpec((1, 1, 2, D), lambda b, i: (b, _is_ctx(i), 0, 0)),
            pl.BlockSpec(w.shape, lambda b, i: (0, 0), pipeline_mode=pl.Buffered(1)),
            pl.BlockSpec((TM, LANE), lambda b, i: (i, 0)),
            pl.BlockSpec((TM, LANE), lambda b, i: (i, 0)),
        ],
        out_specs=[row(n) for n in widths] + [edge, edge],
        out_shape=[jax.ShapeDtypeStruct((B, T, n), F32) for n in widths]
        + [jax.ShapeDtypeStruct((B, nt, HALO, RWKV_COLS), F32)] * 2,
        compiler_params=_cparams(("parallel", "parallel")),
        name="inproj",
    )(x_all, g, mod, w, cos, sin)


RWKV_PARAMS = ("mu", "w0", "w2", "a0", "a2", "g2", "k_k", "k_a", "r_k", "ones_bd")


SCAN_OPERANDS = ("r", "lw", "k", "v", "a", "b")


def _rwkv_tile(z, halo_prev, halo_next, prm, d):
    row = lax.broadcasted_iota(jnp.int32, z.shape, 0)
    prev = jnp.where(row == 0, halo_prev, pltpu.roll(z, 1, 0))
    nxt = jnp.where(row == TM - 1, halo_next, pltpu.roll(z, TM - 1, 0))
    zs = z + prm["mu"][0:1, :] * (prev - z) + prm["mu"][1:2, :] * (nxt - z)
    r = zs[:, 0:RW]
    k = zs[:, RW:2 * RW]
    v = zs[:, 2 * RW:3 * RW]
    wd = zs[:, 3 * RW:3 * RW + LANE]
    ad = zs[:, 3 * RW + LANE:3 * RW + 2 * LANE]
    gd = zs[:, 3 * RW + 2 * LANE:3 * RW + 3 * LANE]
    ones_bd = prm["ones_bd"][...]
    kk = k * prm["k_k"][...]
    sumsq = _seg_sum(kk * kk, ones_bd)
    u = _dot(jnp.tanh(wd).astype(BF16), prm["w2"][d])
    alr = _dot(ad.astype(BF16), prm["a2"][d])
    kk = kk / jnp.maximum(jnp.sqrt(sumsq), 1e-12)
    alr = _sigmoid(prm["a0"][d:d + 1, :] + alr)
    kd = k * (1.0 + (alr - 1.0) * prm["k_a"][...])
    return dict(
        r=r, v=v, a=-kk, k=kd, b=kk * alr, gd=gd,
        lw=-math.exp(-0.5) * _sigmoid(prm["w0"][d:d + 1, :] + u),
        bonus_arg=r * prm["r_k"][...] * kd,
    )


def _wkv_consts(rev):
    t = lax.broadcasted_iota(jnp.int32, (CHUNK, LANE), 0)
    lane = lax.broadcasted_iota(jnp.int32, (CHUNK, LANE), 1)
    s = jnp.bitwise_and(lane, CHUNK - 1)
    tt = lax.broadcasted_iota(jnp.int32, (CHUNK, CHUNK), 0)
    ss = lax.broadcasted_iota(jnp.int32, (CHUNK, CHUNK), 1)
    r2 = lax.broadcasted_iota(jnp.int32, (LANE, LANE), 0)
    l2 = lax.broadcasted_iota(jnp.int32, (LANE, LANE), 1)
    return dict(
        strict=(s > t) if rev else (s < t),
        incl=(s >= t) if rev else (s <= t),
        eye=jnp.where(s == t, 1.0, 0.0).astype(F32),
        m0=lane < CHUNK,
        tri=jnp.where((ss >= tt) if rev else (ss <= tt), 1.0, 0.0).astype(F32),
        bdm=(r2 >= CHUNK) == (l2 >= CHUNK),
        ones=jnp.ones((LANE, LANE), F32),
        levels=[((t >> (j + 1)) == (s >> (j + 1))) & ((t >> j) != (s >> j)) for j in range(int(math.log2(CHUNK)))],
    )


def _bd(y, m0):
    z = jnp.zeros_like(y)
    return jnp.concatenate([jnp.where(m0, y, z), jnp.where(m0, z, y)], axis=0)


def _wkv_prepare_chunks(chunks):
    C = CHUNK
    zero = jnp.zeros((C, LANE), F32)
    n = range(len(chunks))
    cs = [ch[6] for ch in chunks]
    m0 = [c["m0"] for c in cs]
    lw = [ch[1] for ch in chunks]
    v = [ch[3] for ch in chunks]
    lw_hi = [x.astype(BF16).astype(F32) for x in lw]
    lw_lo = [lw[i] - lw_hi[i] for i in n]
    cc = [_dot(cs[i]["tri"], jnp.concatenate([lw_hi[i], lw_lo[i]], axis=1)) for i in n]
    tcol = [_dot_tn(jnp.concatenate([lw_hi[i], lw_lo[i]], axis=0), cs[i]["ones"]) for i in n]
    rt, at, bks, pw_lhs, pw_rhs = [], [], [], [], []
    for i in n:
        r, _, k, _, a, b = chunks[i][:6]
        cum = cc[i][:, :LANE] + cc[i][:, LANE:]
        total = cum[0:1] if chunks[i][7] else cum[C - 1:C]
        p_inv = jnp.exp(-cum)
        p_sc = jnp.exp(total - cum)
        rt.append(r * jnp.exp(cum))
        at.append(a * jnp.exp(cum - lw[i]))
        bks.append(jnp.concatenate([b * p_sc, k * p_sc], axis=0))
        pw_lhs.append(jnp.concatenate([at[i], rt[i]], axis=0))
        pw_rhs.append(jnp.concatenate([_bd(b * p_inv, m0[i]), _bd(k * p_inv, m0[i])], axis=0))
    pw = [_dot_nt(pw_lhs[i], pw_rhs[i]) for i in n]
    ab, akrk, rb, tinv = [], [], [], []
    for i in n:
        ab.append(jnp.where(cs[i]["strict"], pw[i][:C, :LANE], zero))
        akrk.append(jnp.concatenate([jnp.where(cs[i]["strict"], pw[i][:C, LANE:], zero),
                                     jnp.where(cs[i]["incl"], pw[i][C:, LANE:], zero)], axis=0))
        rb.append(jnp.where(cs[i]["incl"], pw[i][C:, :LANE], zero))
        tinv.append(cs[i]["eye"] + jnp.where(cs[i]["levels"][0], ab[i], zero))
    for j in range(1, len(cs[0]["levels"])):
        tl = [_dot(tinv[i], _bd(jnp.where(cs[i]["levels"][j], ab[i], zero), m0[i])) for i in n]
        tinv = [tinv[i] + _dot(tl[i], _bd(tinv[i], m0[i])) for i in n]
    akv = [_dot(akrk[i], _bd(v[i], m0[i])) for i in n]
    wx = [_dot(tinv[i], jnp.concatenate([_bd(at[i], m0[i]), _bd(akv[i][:C], m0[i])], axis=1)) for i in n]
    return [dict(wr=jnp.concatenate([wx[i][:, :LANE], rt[i]], axis=0), x=wx[i][:, LANE:], yk=akv[i][C:], rb=rb[i],
                 bks=bks[i], v=v[i], pcol=jnp.exp(tcol[i])) for i in n]


WKV_PAIRS = RW // LANE


def _wkv_kernel(zf_ref, zb_ref, hpf_ref, hnf_ref, hpb_ref, hnb_ref, *rest):
    prm = dict(zip(RWKV_PARAMS, rest))
    yf_o, yb_o, bonf_o, bonb_o, g_o, h_s = rest[len(RWKV_PARAMS):]

    @pl.when(pl.program_id(1) == 0)
    def _():
        h_s[...] = jnp.zeros_like(h_s)

    C = CHUNK
    nch = TM // C
    tiles = (_rwkv_tile(zf_ref[0], hpf_ref[0, 0], hnf_ref[0, 0], prm, 0),
             _rwkv_tile(zb_ref[0], hpb_ref[0, 0], hnb_ref[0, 0], prm, 1))
    bonus = lambda t: _seg_sum(t["bonus_arg"], prm["ones_bd"][...]) * t["v"]
    fillers = [lambda: g_o.__setitem__(0, _dot(_sigmoid(tiles[0]["gd"]).astype(BF16), prm["g2"][...])),
               lambda: bonf_o.__setitem__(0, bonus(tiles[0])),
               lambda: bonb_o.__setitem__(0, bonus(tiles[1]))]
    dirs =((False, tiles[0], yf_o), (True, tiles[1], yb_o))
    order = {False: list(range(nch)), True: list(range(nch - 1, -1, -1))}
    chains = [(rev, tile, y_o, slice(p * LANE, (p + 1) * LANE)) for rev, tile, y_o in dirs for p in range(WKV_PAIRS)]
    consts = {rev: _wkv_consts(rev) for rev, _, _ in dirs}
    chunks = []
    for rev, tile, _, lanes in chains:
        for c in order[rev]:
            chunks.append(tuple(tile[name][c * C:(c + 1) * C, lanes] for name in SCAN_OPERANDS) + (consts[rev], rev))
    prepared = _wkv_prepare_chunks(chunks)
    m0 = consts[False]["m0"]
    bdm = consts[False]["bdm"]
    nc = range(len(chains))
    states = [h_s[j] for j in nc]
    for step in range(nch):
        ps = [prepared[j * nch + step] for j in nc]
        s1 = [_dot(ps[j]["wr"], states[j]) for j in nc]
        if step < len(fillers):
            fillers[step]()
        u = [s1[j][:C] + ps[j]["x"] for j in nc]
        s2 = [_dot(ps[j]["rb"], _bd(u[j], m0)) for j in nc]
        s3 = [_dot_tn(ps[j]["bks"], jnp.concatenate([u[j], ps[j]["v"]], axis=0)) for j in nc]
        for j, (rev, _, y_o, lanes) in enumerate(chains):
            c = order[rev][step]
            y_o[0, c * C:(c + 1) * C, lanes] = s1[j][C:] + s2[j] + ps[j]["yk"]
        states = [ps[j]["pcol"] * states[j] + jnp.where(bdm, s3[j], jnp.zeros_like(s3[j])) for j in nc]
    for j in nc:
        h_s[j] = states[j]


def _wkv_scan(za, hprev, hnext, p):
    B, T, _ = za.shape
    nt = T // TM
    rev_tile = lambda j: jnp.where(j == 0, 0, nt - j)
    tile = lambda n, idx: pl.BlockSpec((1, TM, n), lambda b, i: (b, idx(i), 0))
    halo = lambda idx: pl.BlockSpec((1, 1, 1, RWKV_COLS), lambda b, i: (b, idx(i), 0, 0))
    full = lambda a: pl.BlockSpec(a.shape, lambda b, i: (0,) * a.ndim)
    fwd = lambda i: i
    params = [p[name] for name in RWKV_PARAMS]
    return pl.pallas_call(
        _wkv_kernel,
        grid=(B, nt),
        in_specs=[tile(RWKV_COLS, fwd), tile(RWKV_COLS, rev_tile), halo(fwd), halo(fwd), halo(rev_tile),
                  halo(rev_tile)] + [full(a) for a in params],
        out_specs=[tile(RW, fwd), tile(RW, rev_tile), tile(RW, fwd), tile(RW, rev_tile), tile(RW, fwd)],
        out_shape=[jax.ShapeDtypeStruct((B, T, RW), F32)] * 5,
        scratch_shapes=[pltpu.VMEM((2 * WKV_PAIRS, LANE, LANE), F32)],
        compiler_params=_cparams(("parallel", "arbitrary")),
        name="wkv_scan",
    )(za, za, hprev, hnext, hprev, hnext, *params)


def _load_vt(v_ref, vt_s):
    for c in range(v_ref.shape[1] // TK):
        vt_s[:, c * TK:(c + 1) * TK] = v_ref[0, c * TK:(c + 1) * TK, :].T.astype(BF16)


def _key_chunks(n_keys, ctx_only):
    chunks = [(0, CTX)]
    if not ctx_only:
        big = min(KEY_BLOCK, n_keys - CTX)
        chunks += [(CTX + j * big, big) for j in range((n_keys - CTX) // big)]
    return chunks


def _attend(streams, k_ref, vt_s, chunks):
    state = [(jnp.full((1, TM), -1e30, F32), jnp.zeros((1, TM), F32),
              jnp.zeros((rows.stop - rows.start, TM), F32)) for _, _, rows in streams]
    subs = [[(start + t, min(TK, size - t)) for t in range(0, size, TK)] for start, size in chunks]

    ns = range(len(streams))

    def score(c, t, j):
        st, sz = subs[c][t]
        q, lanes, _ = streams[j]
        s = _dot_nt(k_ref[0, st:st + sz, lanes], q)
        return s, jnp.max(s, axis=0, keepdims=True)

    cur = [[score(0, t, j) for j in ns] for t in range(len(subs[0]))]
    for c in range(len(chunks)):
        n_next = len(subs[c + 1]) if c + 1 < len(chunks) else 0
        nxt = [[] for _ in range(n_next)]
        new_state = []
        for j in ns:
            m, l, acc = state[j]
            m_new = functools.reduce(jnp.maximum, [row[j][1] for row in cur], m)
            alpha = jnp.exp2(m - m_new)
            new_state.append((m_new, alpha * l, alpha * acc))
        for t, (st, sz) in enumerate(subs[c]):
            for j, (_, _, rows) in enumerate(streams):
                if t < n_next:
                    nxt[t].append(score(c + 1, t, j))
                m_new, l, acc = new_state[j]
                p = jnp.exp2(cur[t][j][0] - m_new)
                new_state[j] = (m_new, l + jnp.sum(p, axis=0, keepdims=True),
                                acc + _dot(vt_s[rows, st:st + sz], p.astype(BF16)))
        for t in range(len(subs[c]), n_next):
            nxt[t] = [score(c + 1, t, j) for j in ns]
        state = new_state
        cur = nxt
    return [(acc, l) for _, l, acc in state]


LAG_BLOCK = 4096
SAFE_EXCESS = 64.0


def _attend_lagged(streams, k_ref, vt_s):
    n_keys = k_ref.shape[1]
    ns = range(len(streams))
    m, l, acc = [], [], []
    for q, lanes, rows in streams:
        s = _dot_nt(k_ref[0, 0:CTX, lanes], q)
        mj = jnp.max(s, axis=0, keepdims=True)
        p = jnp.exp2(s - mj)
        m.append(mj)
        l.append(jnp.sum(p, axis=0, keepdims=True))
        acc.append(_dot(vt_s[rows, 0:CTX], p.astype(BF16)))
    score = lambda st, sz, j: _dot_nt(k_ref[0, st:st + sz, streams[j][1]], streams[j][0])
    excess = jnp.zeros((1, TM), F32)
    big = min(LAG_BLOCK, n_keys - CTX)
    for start in range(CTX, n_keys, big):
        subs = [(start + t, min(TK, big - t)) for t in range(0, big, TK)]
        top = list(m)
        cur = [score(*subs[0], j) for j in ns]
        for t, (st, sz) in enumerate(subs):
            nxt = []
            for j in ns:
                if t + 1 < len(subs):
                    nxt.append(score(*subs[t + 1], j))
                s = cur[j]
                top[j] = jnp.maximum(top[j], jnp.max(s, axis=0, keepdims=True))
                p = jnp.exp2(s - m[j])
                l[j] = l[j] + jnp.sum(p, axis=0, keepdims=True)
                acc[j] = acc[j] + _dot(vt_s[streams[j][2], st:st + sz], p.astype(BF16))
            cur = nxt
        for j in ns:
            excess = jnp.maximum(excess, top[j] - m[j])
            alpha = jnp.exp2(m[j] - top[j])
            m[j], l[j], acc[j] = top[j], alpha * l[j], alpha * acc[j]
    return [(acc[j], l[j]) for j in ns], excess


def _run_attention(i, streams, k_ref, vt_s, finish):
    n_keys = k_ref.shape[1]

    @pl.when(i == 0)
    def _():
        finish(_attend(streams, k_ref, vt_s, _key_chunks(n_keys, True)))

    @pl.when(i > 0)
    def _():
        res, excess = _attend_lagged(streams, k_ref, vt_s)
        finish(res)

        @pl.when(jnp.max(excess) > SAFE_EXCESS)
        def _():
            finish(_attend(streams, k_ref, vt_s, _key_chunks(n_keys, False)))


DIFF_HEADS = 2


def _diff_attn_kernel(lam_init, q_ref, k_ref, v_ref, lamv_ref, g_ref, o_ref, vt_s):
    i = pl.program_id(2)

    @pl.when(i == 0)
    def _():
        _load_vt(v_ref, vt_s)

    lv = lamv_ref[...]
    lam = (jnp.exp(jnp.sum(lv[0:1] * lv[1:2], axis=-1, keepdims=True))
           - jnp.exp(jnp.sum(lv[2:3] * lv[3:4], axis=-1, keepdims=True)) + lam_init)
    lane = lax.broadcasted_iota(jnp.int32, (TM, LANE), 1)
    zq = jnp.zeros((TM, LANE), F32)
    heads = [slice(h * LANE, (h + 1) * LANE) for h in range(DIFF_HEADS)]
    streams = []
    for lanes in heads:
        q = q_ref[0, :, lanes] * (DIFF_HD ** -0.5 * LOG2E)
        streams += [(jnp.where(lane < DIFF_HD, q, zq), lanes, lanes), (jnp.where(lane < DIFF_HD, zq, q), lanes, lanes)]

    def finish(res):
        for h, lanes in enumerate(heads):
            (a1, l1), (a2, l2) = res[2 * h], res[2 * h + 1]
            o = a1 / l1 - lam * (a2 / l2)
            ms = jnp.mean(o * o, axis=0, keepdims=True)
            o = o * lax.rsqrt(ms + NORM_EPS)
            o_ref[0, :, lanes] = o.T * g_ref[...] * (1.0 - lam_init)

    _run_attention(i, streams, k_ref, vt_s, finish)


def _diff_attention(zb, lam_vecs, subln_g, layer_idx):
    B, T, _ = zb.shape
    nt = T // TM
    lam_init = 0.8 - 0.6 * math.exp(-0.3 * layer_idx)
    width = DIFF_HEADS * LANE
    groups = H_B // DIFF_HEADS
    return pl.pallas_call(
        functools.partial(_diff_attn_kernel, lam_init),
        grid=(B, groups, nt),
        in_specs=[
            pl.BlockSpec((1, TM, width), lambda b, h, i: (b, i, h)),
            pl.BlockSpec((1, T, width), lambda b, h, i: (b, 0, groups + h)),
            pl.BlockSpec((1, T, width), lambda b, h, i: (b, 0, 2 * groups + h)),
            pl.BlockSpec((4, DIFF_HD), lambda b, h, i: (0, 0)),
            pl.BlockSpec((1, LANE), lambda b, h, i: (0, 0)),
        ],
        out_specs=pl.BlockSpec((1, TM, width), lambda b, h, i: (b, i, h)),
        out_shape=jax.ShapeDtypeStruct((B, T, H_B * LANE), F32),
        scratch_shapes=[pltpu.VMEM((width, T), BF16)],
        compiler_params=_cparams(("parallel", "parallel", "arbitrary")),
        name="diff_attn",
    )(zb, zb, zb, lam_vecs, subln_g)


def _mla_proj_kernel(z_ref, qg_ref, kvg_ref, wq_ref, wqr_ref, wk_ref, wv_ref, cq_ref, sq_ref, ck_ref, sk_ref,
                     q_o, k_o, v_o):
    z = z_ref[0]
    cq = z[:, 0:Q_LORA]
    ckv = z[:, Q_LORA:Q_LORA + KV_LORA]
    kr = z[:, 5 * LANE:6 * LANE] * ck_ref[...] + z[:, 6 * LANE:7 * LANE] * sk_ref[...]
    rms = lambda t: t * lax.rsqrt(jnp.mean(t * t, axis=-1, keepdims=True) + NORM_EPS)
    cqn = (rms(cq) * qg_ref[...]).astype(BF16)
    ckvn = rms(ckv) * kvg_ref[...]
    q = _dot(cqn, wq_ref[...])
    qr = _dot(cqn, wqr_ref[...])
    cos = cq_ref[...]
    sin = sq_ref[...]
    scale = (MLA_NOPE + MLA_ROPE) ** -0.5 * LOG2E
    for h in range(H_C):
        sl = slice(h * LANE, (h + 1) * LANE)
        q_o[0, :, sl] = (q[:, sl] * cos + qr[:, sl] * sin) * scale
    kin = jnp.concatenate([ckvn, kr], axis=1).astype(BF16)
    k_o[0] = _dot(kin, wk_ref[...])
    v_o[0] = _dot(ckvn.astype(BF16), wv_ref[...])


def _mla_project(zc, p):
    B, T, _ = zc.shape
    full = lambda a: pl.BlockSpec(a.shape, lambda b, i: (0,) * a.ndim)
    tab = pl.BlockSpec((TM, LANE), lambda b, i: (i, 0))
    ws = [p["q_g"], p["kv_g"], p["w_uq"], p["w_uq_rot"], p["w_uk"], p["w_uv"]]
    return pl.pallas_call(
        _mla_proj_kernel,
        grid=(B, T // TM),
        in_specs=[pl.BlockSpec((1, TM, MLA_PAD), lambda b, i: (b, i, 0))] + [full(a) for a in ws] + [tab] * 4,
        out_specs=[
            pl.BlockSpec((1, TM, H_C * LANE), lambda b, i: (b, i, 0)),
            pl.BlockSpec((1, TM, H_C * LANE), lambda b, i: (b, i, 0)),
            pl.BlockSpec((1, TM, H_C * MLA_V), lambda b, i: (b, i, 0)),
        ],
        out_shape=[
            jax.ShapeDtypeStruct((B, T, H_C * LANE), F32),
            jax.ShapeDtypeStruct((B, T, H_C * LANE), F32),
            jax.ShapeDtypeStruct((B, T, H_C * MLA_V), F32),
        ],
        compiler_params=_cparams(("parallel", "parallel")),
        name="mla_project",
    )(zc, *ws, p["cos_q"], p["sin_q"], p["cos_k"], p["sin_k"])


MLA_HEADS = 4


def _mla_attn_kernel(q_ref, k_ref, v_ref, o_ref, vt_s):
    i = pl.program_id(2)

    @pl.when(i == 0)
    def _():
        _load_vt(v_ref, vt_s)

    streams = [(q_ref[0, :, h * LANE:(h + 1) * LANE], slice(h * LANE, (h + 1) * LANE),
                slice(h * MLA_V, (h + 1) * MLA_V)) for h in range(MLA_HEADS)]

    def finish(res):
        for p in range(MLA_HEADS // 2):
            (a0, l0), (a1, l1) = res[2 * p], res[2 * p + 1]
            o_ref[0, :, p * LANE:(p + 1) * LANE] = jnp.concatenate([a0 / l0, a1 / l1], axis=0).T

    _run_attention(i, streams, k_ref, vt_s, finish)


def _mla_attention(q, k, v):
    B, T, _ = q.shape
    nt = T // TM
    return pl.pallas_call(
        _mla_attn_kernel,
        grid=(B, H_C // MLA_HEADS, nt),
        in_specs=[
            pl.BlockSpec((1, TM, MLA_HEADS * LANE), lambda b, p, i: (b, i, p)),
            pl.BlockSpec((1, T, MLA_HEADS * LANE), lambda b, p, i: (b, 0, p)),
            pl.BlockSpec((1, T, MLA_HEADS * MLA_V), lambda b, p, i: (b, 0, p)),
        ],
        out_specs=pl.BlockSpec((1, TM, MLA_HEADS * MLA_V), lambda b, p, i: (b, i, p)),
        out_shape=jax.ShapeDtypeStruct((B, T, H_C * MLA_V), F32),
        scratch_shapes=[pltpu.VMEM((MLA_HEADS * MLA_V, T), BF16)],
        compiler_params=_cparams(("parallel", "parallel", "arbitrary")),
        name="mla_attn",
    )(q, k, v)


def _merge_kernel(x_ref, yf_ref, yb_ref, bonf_ref, bonb_ref, g_ref, yd_ref, yc_ref, zg_ref, g1_ref, lnw_ref,
                  lnb_ref, ones_ref, wb_ref, wo_ref, o_ref):
    ones_bd = ones_ref[...]
    y = yf_ref[0] + yb_ref[0]
    mean = _seg_sum(y, ones_bd) * (1.0 / N_A)
    yc = y - mean
    var = _seg_sum(yc * yc, ones_bd) * (1.0 / N_A)
    bonus = bonf_ref[0] + bonb_ref[0]
    ya = (yc * lax.rsqrt(var + GN_EPS) * lnw_ref[...] + lnb_ref[...] + bonus) * g_ref[0]
    acc = None
    for j, yj in enumerate((ya, yd_ref[0], yc_ref[0])):
        gate = _sigmoid(zg_ref[0, :, j * D:(j + 1) * D])
        t = gate * _dot(yj.astype(BF16), wb_ref[j])
        acc = t if acc is None else acc + t
    o_ref[0] = x_ref[0] + g1_ref[0, 0] * _dot(acc.astype(BF16), wo_ref[...])


def _merge(x_all, yf, yb, bonf, bonb, g, yd, yc, zg, g1, lnw, lnb, ones_bd, wb, wo, latent_only):
    B, T, _ = x_all.shape
    skip = 1 if latent_only else 0
    row = lambda n: pl.BlockSpec((1, TM, n), lambda b, i: (b, i + skip, 0))
    full = lambda a: pl.BlockSpec(a.shape, lambda b, i: (0,) * a.ndim)
    return pl.pallas_call(
        _merge_kernel,
        grid=(B, T // TM - skip),
        in_specs=[row(D)] + [row(RW)] * 7 + [row(3 * D),
                  pl.BlockSpec((1, 1, 1, D), lambda b, i: (b, _is_ctx(i + skip), 0, 0)),
                  full(lnw), full(lnb), full(ones_bd), full(wb), full(wo)],
        out_specs=pl.BlockSpec((1, TM, D), lambda b, i: (b, i, 0)),
        out_shape=jax.ShapeDtypeStruct((B, T - skip * TM, D), F32),
        compiler_params=_cparams(("parallel", "parallel")),
        name="merge",
    )(x_all, yf, yb, bonf, bonb, g, yd, yc, zg, g1, lnw, lnb, ones_bd, wb, wo)


def _ffn_kernel(final, x_ref, g_ref, mod_ref, wg_ref, wu_ref, wd_ref, fg_ref, o_ref):
    x = x_ref[0]
    h = _norm_mod(x, g_ref[...], mod_ref[0, 0, 0:1, :], mod_ref[0, 0, 1:2, :]).astype(BF16)
    gate = _dot(h, wg_ref[...])
    up = _dot(h, wu_ref[...])
    act = (gate * _sigmoid(gate) * up).astype(BF16)
    out = x + mod_ref[0, 0, 2:3, :] * _dot(act, wd_ref[...])
    if final:
        ms = jnp.mean(out * out, axis=-1, keepdims=True)
        out = out * lax.rsqrt(ms + NORM_EPS) * fg_ref[...]
    o_ref[0] = out


def _ffn(x_all, g, mod, wg, wu, wd, fg, final):
    B, T, _ = x_all.shape
    full = lambda a: pl.BlockSpec(a.shape, lambda b, i: (0,) * a.ndim, pipeline_mode=pl.Buffered(1))
    return pl.pallas_call(
        functools.partial(_ffn_kernel, final),
        grid=(B, T // TM),
        in_specs=[
            pl.BlockSpec((1, TM, D), lambda b, i: (b, i, 0)),
            pl.BlockSpec((1, D), lambda b, i: (0, 0)),
            pl.BlockSpec((1, 1, 3, D), lambda b, i: (b, 0 if final else _is_ctx(i), 0, 0)),
            full(wg), full(wu), full(wd),
            pl.BlockSpec((1, D), lambda b, i: (0, 0)),
        ],
        out_specs=pl.BlockSpec((1, TM, D), lambda b, i: (b, i, 0)),
        out_shape=jax.ShapeDtypeStruct((B, T, D), F32),
        compiler_params=_cparams(("parallel", "parallel")),
        name="ffn",
    )(x_all, g, mod, wg, wu, wd, fg)


def _rot_cols(w, half):
    d, n = w.shape
    w4 = w.reshape(d, n // (2 * half), 2, half)
    return jnp.concatenate([-w4[:, :, 1:2], w4[:, :, 0:1]], axis=2).reshape(d, n)


def _rope_tables(n_lat):
    rows = n_lat // GRID_W
    row = jnp.repeat(jnp.arange(rows, dtype=F32), GRID_W)
    col = jnp.tile(jnp.arange(GRID_W, dtype=F32), rows)

    def tables(rot_dim):
        n_freq = rot_dim // 4
        inv_freq = ROPE_BASE ** (-jnp.arange(n_freq, dtype=F32) / n_freq)
        ang = jnp.concatenate([row[:, None] * inv_freq, col[:, None] * inv_freq], axis=-1)
        cos = jnp.concatenate([jnp.ones((CTX, rot_dim // 2), F32), jnp.cos(ang)], axis=0)
        sin = jnp.concatenate([jnp.zeros((CTX, rot_dim // 2), F32), jnp.sin(ang)], axis=0)
        return cos, sin

    T = CTX + n_lat
    cos_b, sin_b = tables(DIFF_HD)
    cos_c, sin_c = tables(MLA_ROPE)
    one = lambda n: jnp.ones((T, n), F32)
    zero = lambda n: jnp.zeros((T, n), F32)
    return dict(
        cos_b=jnp.tile(cos_b, (1, 4)), sin_b=jnp.tile(sin_b, (1, 4)),
        cos_q=jnp.concatenate([one(MLA_NOPE), cos_c, cos_c, one(32)], axis=1),
        sin_q=jnp.concatenate([zero(MLA_NOPE), sin_c, sin_c, zero(32)], axis=1),
        cos_k=jnp.concatenate([cos_c, cos_c, one(96)], axis=1),
        sin_k=jnp.concatenate([sin_c, sin_c, zero(96)], axis=1),
    )


def _pad_heads(w, per_head, n_heads):
    d = w.shape[0]
    w3 = w.reshape(d, n_heads, per_head)
    return jnp.pad(w3, ((0, 0), (0, 0), (0, LANE - per_head))).reshape(d, n_heads * LANE)


def _layer_weights(l, ws):
    (w_in, mu, w0, w2, a0, a2, g2, k_k, k_a, r_k, ln_w, ln_b, q_g, kv_g, w_uq, w_uk, w_uv, w_branch, w_out,
     wg, wu, wd) = [w[l] for w in ws]
    o1, o2, o3 = RWKV_COLS, RWKV_COLS + DIFF_COLS, RWKV_COLS + DIFF_COLS + MLA_COLS
    w_a, w_b, w_c, w_g = w_in[:, :o1], w_in[:, o1:o2], w_in[:, o2:o3], w_in[:, o3:]
    w_kr = w_c[:, Q_LORA + KV_LORA:]
    w_c_ext = jnp.concatenate([w_c, jnp.zeros((D, 96), F32), _rot_cols(w_kr, MLA_ROPE // 2),
                               jnp.zeros((D, 96), F32)], axis=1)
    zpad = jnp.zeros((2, 64, RW), F32)
    seg = jnp.arange(RW) // N_A
    w_uq_p = _pad_heads(w_uq, MLA_NOPE + MLA_ROPE, H_C)
    w_uq_rope = w_uq.reshape(Q_LORA, H_C, MLA_NOPE + MLA_ROPE)[:, :, MLA_NOPE:].reshape(Q_LORA, H_C * MLA_ROPE)
    w_uq_rot = _rot_cols(w_uq_rope, MLA_ROPE // 2).reshape(Q_LORA, H_C, MLA_ROPE)
    w_uq_rot_p = jnp.pad(w_uq_rot, ((0, 0), (0, 0), (MLA_NOPE, LANE - MLA_NOPE - MLA_ROPE))).reshape(Q_LORA, H_C * LANE)
    place = jnp.zeros((LANE, H_C, LANE), F32).at[
        jnp.arange(MLA_ROPE)[:, None], jnp.arange(H_C)[None, :], MLA_NOPE + jnp.arange(MLA_ROPE)[:, None]].set(1.0)
    w_uk_p = jnp.concatenate([_pad_heads(w_uk, MLA_NOPE, H_C), place.reshape(LANE, H_C * LANE)], axis=0)
    return dict(
        w_in=jnp.concatenate([w_a, w_c_ext, w_g, w_b], axis=1).astype(BF16),
        rwkv=dict(
            mu=mu, w0=w0, a0=a0,
            w2=jnp.stack([jnp.concatenate([w2[0], zpad[0]], 0), jnp.concatenate([zpad[1], w2[1]], 0)]).astype(BF16),
            a2=jnp.stack([jnp.concatenate([a2[0], zpad[0]], 0), jnp.concatenate([zpad[1], a2[1]], 0)]).astype(BF16),
            g2=g2.astype(BF16), k_k=k_k[None], k_a=k_a[None], r_k=r_k.reshape(1, RW),
            ones_bd=(seg[:, None] == seg[None, :]).astype(BF16)),
        ln_w=ln_w[None], ln_b=ln_b[None],
        mla=dict(q_g=q_g[None], kv_g=kv_g[None], w_uq=w_uq_p.astype(BF16), w_uq_rot=w_uq_rot_p.astype(BF16),
                 w_uk=w_uk_p.astype(BF16), w_uv=w_uv.astype(BF16)),
        w_branch=w_branch.astype(BF16), w_out=w_out.astype(BF16),
        wg=wg.astype(BF16), wu=wu.astype(BF16), wd=wd.astype(BF16),
    )


def _mod_rows(m, idx, B):
    lat = m[:B][:, idx]
    ctx = jnp.broadcast_to(m[B][idx][None], lat.shape)
    return jnp.stack([lat, ctx], axis=1)


def kernel(x, c, ctx, c_ctx, ada_w, ada_b, norm1_g, norm2_g, w_in, rwkv_shift_mu, rwkv_w0, rwkv_w2, rwkv_a0,
           rwkv_a2, rwkv_g2, rwkv_k_k, rwkv_k_a, rwkv_r_k, rwkv_ln_w, rwkv_ln_b, diff_lambda, diff_subln_g,
           mla_q_norm_g, mla_kv_norm_g, mla_w_uq, mla_w_uk, mla_w_uv, w_branch, w_out, ffn_w_gate, ffn_w_up,
           ffn_w_down, final_norm_g):
    B, S, _ = x.shape
    L = ada_w.shape[0]
    T = CTX + S
    nt = T // TM
    x_all = jnp.concatenate([ctx, x], axis=1)
    c8 = jnp.concatenate([c, c_ctx[None], jnp.zeros((8 - B - 1, D), F32)], axis=0)
    mod = _modulation(c8, ada_w, ada_b).reshape(L, 8, 6, D)
    tabs = _rope_tables(S)
    ws = (w_in, rwkv_shift_mu, rwkv_w0, rwkv_w2, rwkv_a0, rwkv_a2, rwkv_g2, rwkv_k_k, rwkv_k_a, rwkv_r_k,
          rwkv_ln_w, rwkv_ln_b, mla_q_norm_g, mla_kv_norm_g, mla_w_uq, mla_w_uk, mla_w_uv, w_branch, w_out,
          ffn_w_gate, ffn_w_up, ffn_w_down)
    for l in range(L):
        p = _layer_weights(l, ws)
        mod1 = _mod_rows(mod[l], jnp.array([0, 1]), B)
        g1 = _mod_rows(mod[l], jnp.array([2]), B)
        mod2 = _mod_rows(mod[l], jnp.array([3, 4, 5]), B)
        n1 = norm1_g[l][None]

        za, zc, zg, zb, first8, last8 = _inproj(x_all, n1, mod1, p["w_in"], tabs["cos_b"], tabs["sin_b"])

        zeros = lambda n: jnp.zeros((B, n, RWKV_COLS), F32)
        last_rows, first_rows = last8[:, :, HALO - 1], first8[:, :, 0]
        hprev = jnp.concatenate([zeros(2), last_rows[:, 1:nt - 1]], axis=1)[:, :, None]
        hnext = jnp.concatenate([zeros(1), first_rows[:, 2:nt], zeros(1)], axis=1)[:, :, None]
        yf, yb, bonf, bonb, g = _wkv_scan(za, hprev, hnext, p["rwkv"])

        yd = _diff_attention(zb, diff_lambda[l], diff_subln_g[l][None], l)

        q, k, vm = _mla_project(zc, dict(p["mla"], cos_q=tabs["cos_q"], sin_q=tabs["sin_q"],
                                         cos_k=tabs["cos_k"], sin_k=tabs["sin_k"]))
        yc = _mla_attention(q, k, vm)

        last = l == L - 1
        x_all = _merge(x_all, yf, yb, bonf, bonb, g, yd, yc, zg, g1, p["ln_w"], p["ln_b"], p["rwkv"]["ones_bd"],
                       p["w_branch"], p["w_out"], last)
        x_all = _ffn(x_all, norm2_g[l][None], mod2, p["wg"], p["wu"], p["wd"], final_norm_g[None], last)
    return x_all
```

```python
import functools
import math

import jax
import jax.numpy as jnp
from jax import lax
from jax.experimental import pallas as pl
from jax.experimental.pallas import tpu as pltpu

F32 = jnp.float32
BF16 = jnp.bfloat16

D = 1024
CTX = 256
GRID_W = 64
ROPE_BASE = 10000.0
NORM_EPS = 1e-6
GN_EPS = 64e-5

H_A, N_A = 8, 64
RW = H_A * N_A
RWKV_COLS = 3 * RW + 2 * 64 + 2 * 64 + 128
H_B, DIFF_HD = 4, 64
DIFF_QK = H_B * 2 * DIFF_HD
DIFF_COLS = 3 * DIFF_QK
H_C = 8
Q_LORA, KV_LORA = 384, 256
MLA_NOPE, MLA_ROPE, MLA_V = 64, 32, 64
MLA_COLS = Q_LORA + KV_LORA + MLA_ROPE
MLA_PAD = 896
D_FF = 2816

TM = 256
LANE = 128
CHUNK = 64
TK = 256
KEY_BLOCK = 2048
LOG2E = 1.4426950408889634
VMEM_LIMIT = 56 * 1024 * 1024


def _cparams(sem):
    return pltpu.CompilerParams(dimension_semantics=sem, vmem_limit_bytes=VMEM_LIMIT)


def _sigmoid(x):
    return 1.0 / (1.0 + jnp.exp(-x))


def _dot(a, b):
    return jnp.dot(a, b, preferred_element_type=F32)


def _dot_nt(a, b):
    return lax.dot_general(a, b, (((1,), (1,)), ((), ())), preferred_element_type=F32)


def _dot_tn(a, b):
    return lax.dot_general(a, b, (((0,), (0,)), ((), ())), preferred_element_type=F32)


def _seg_sum(x, ones_bd):
    return _dot(x.astype(BF16), ones_bd)


def _is_ctx(i):
    return jnp.where(i == 0, 1, 0)


def _mod_kernel(c_ref, w_ref, b_ref, o_ref):
    c = c_ref[...]
    s = (c * _sigmoid(c)).astype(BF16)
    o_ref[0] = _dot(s, w_ref[0].astype(BF16)) + b_ref[0]


def _modulation(c8, ada_w, ada_b):
    L = ada_w.shape[0]
    tn = 1536
    return pl.pallas_call(
        _mod_kernel,
        grid=(L, 6 * D // tn),
        in_specs=[
            pl.BlockSpec((8, D), lambda l, j: (0, 0)),
            pl.BlockSpec((1, D, tn), lambda l, j: (l, 0, j)),
            pl.BlockSpec((1, 1, tn), lambda l, j: (l, 0, j)),
        ],
        out_specs=pl.BlockSpec((1, 8, tn), lambda l, j: (l, 0, j)),
        out_shape=jax.ShapeDtypeStruct((L, 8, 6 * D), F32),
        compiler_params=_cparams(("parallel", "parallel")),
        name="adaln_mod",
    )(c8, ada_w, ada_b[:, None, :])


def _norm_mod(x, g, shift, scale):
    ms = jnp.mean(x * x, axis=-1, keepdims=True)
    return (x * lax.rsqrt(ms + NORM_EPS)) * g * (1.0 + scale) + shift


N_ROPE = 2 * DIFF_QK
IN_SPLITS = (RWKV_COLS, MLA_PAD, 3 * D, DIFF_COLS)
HALO = 8


def _inproj_kernel(x_ref, g_ref, mod_ref, w_ref, cos_ref, sin_ref, za_o, zc_o, zg_o, zb_o, first_o, last_o):
    h = _norm_mod(x_ref[0], g_ref[...], mod_ref[0, 0, 0:1, :], mod_ref[0, 0, 1:2, :]).astype(BF16)
    ofs = [0]
    for n in IN_SPLITS:
        ofs.append(ofs[-1] + n)
    proj = lambda j: _dot(h, w_ref[:, ofs[j]:ofs[j + 1]])
    za = proj(0)
    za_o[0] = za
    first_o[0, 0] = za[0:HALO]
    last_o[0, 0] = za[TM - HALO:TM]
    zc_o[0] = proj(1)
    zg_o[0] = proj(2)
    z = proj(3)
    cos = cos_ref[...]
    sin = sin_ref[...]
    half = DIFF_HD // 2
    first_half = jnp.bitwise_and(lax.broadcasted_iota(jnp.int32, (TM, LANE), 1), DIFF_HD - 1) < half
    for j in range(N_ROPE // LANE):
        sl = slice(j * LANE, (j + 1) * LANE)
        zj = z[:, sl]
        rot = jnp.where(first_half, -pltpu.roll(zj, LANE - half, 1), pltpu.roll(zj, half, 1))
        zb_o[0, :, sl] = zj * cos + rot * sin
    zb_o[0, :, N_ROPE:] = z[:, N_ROPE:]


def _inproj(x_all, g, mod, w, cos, sin):
    B, T, _ = x_all.shape
    nt = T // TM
    row = lambda n: pl.BlockSpec((1, TM, n), lambda b, i: (b, i, 0))
    edge = pl.BlockSpec((1, 1, HALO, RWKV_COLS), lambda b, i: (b, i, 0, 0))
    widths = (RWKV_COLS, MLA_PAD, 3 * D, DIFF_COLS)
    return pl.pallas_call(
        _inproj_kernel,
        grid=(B, nt),
        in_specs=[
            row(D),
            pl.BlockSpec((1, D), lambda b, i: (0, 0)),
            pl.BlockSpec((1, 1, 2, D), lambda b, i: (b, _is_ctx(i), 0, 0)),
            pl.BlockSpec(w.shape, lambda b, i: (0, 0), pipeline_mode=pl.Buffered(1)),
            pl.BlockSpec((TM, LANE), lambda b, i: (i, 0)),
            pl.BlockSpec((TM, LANE), lambda b, i: (i, 0)),
        ],
        out_specs=[row(n) for n in widths] + [edge, edge],
        out_shape=[jax.ShapeDtypeStruct((B, T, n), F32) for n in widths]
        + [jax.ShapeDtypeStruct((B, nt, HALO, RWKV_COLS), F32)] * 2,
        compiler_params=_cparams(("parallel", "parallel")),
        name="inproj",
    )(x_all, g, mod, w, cos, sin)


RWKV_PARAMS = ("mu", "w0", "w2", "a0", "a2", "g2", "k_k", "k_a", "r_k", "ones_bd")


SCAN_OPERANDS = ("r", "lw", "k", "v", "a", "b")


def _rwkv_tile(z, halo_prev, halo_next, prm, d):
    row = lax.broadcasted_iota(jnp.int32, z.shape, 0)
    prev = jnp.where(row == 0, halo_prev, pltpu.roll(z, 1, 0))
    nxt = jnp.where(row == TM - 1, halo_next, pltpu.roll(z, TM - 1, 0))
    zs = z + prm["mu"][0:1, :] * (prev - z) + prm["mu"][1:2, :] * (nxt - z)
    r = zs[:, 0:RW]
    k = zs[:, RW:2 * RW]
    v = zs[:, 2 * RW:3 * RW]
    wd = zs[:, 3 * RW:3 * RW + LANE]
    ad = zs[:, 3 * RW + LANE:3 * RW + 2 * LANE]
    gd = zs[:, 3 * RW + 2 * LANE:3 * RW + 3 * LANE]
    ones_bd = prm["ones_bd"][...]
    kk = k * prm["k_k"][...]
    sumsq = _seg_sum(kk * kk, ones_bd)
    u = _dot(jnp.tanh(wd).astype(BF16), prm["w2"][d])
    alr = _dot(ad.astype(BF16), prm["a2"][d])
    kk = kk / jnp.maximum(jnp.sqrt(sumsq), 1e-12)
    alr = _sigmoid(prm["a0"][d:d + 1, :] + alr)
    kd = k * (1.0 + (alr - 1.0) * prm["k_a"][...])
    return dict(
        r=r, v=v, a=-kk, k=kd, b=kk * alr, gd=gd,
        lw=-math.exp(-0.5) * _sigmoid(prm["w0"][d:d + 1, :] + u),
        bonus_arg=r * prm["r_k"][...] * kd,
    )


def _wkv_consts(rev):
    t = lax.broadcasted_iota(jnp.int32, (CHUNK, LANE), 0)
    lane = lax.broadcasted_iota(jnp.int32, (CHUNK, LANE), 1)
    s = jnp.bitwise_and(lane, CHUNK - 1)
    tt = lax.broadcasted_iota(jnp.int32, (CHUNK, CHUNK), 0)
    ss = lax.broadcasted_iota(jnp.int32, (CHUNK, CHUNK), 1)
    r2 = lax.broadcasted_iota(jnp.int32, (LANE, LANE), 0)
    l2 = lax.broadcasted_iota(jnp.int32, (LANE, LANE), 1)
    return dict(
        strict=(s > t) if rev else (s < t),
        incl=(s >= t) if rev else (s <= t),
        eye=jnp.where(s == t, 1.0, 0.0).astype(F32),
        m0=lane < CHUNK,
        tri=jnp.where((ss >= tt) if rev else (ss <= tt), 1.0, 0.0).astype(F32),
        bdm=(r2 >= CHUNK) == (l2 >= CHUNK),
        ones=jnp.ones((LANE, LANE), F32),
        levels=[((t >> (j + 1)) == (s >> (j + 1))) & ((t >> j) != (s >> j)) for j in range(int(math.log2(CHUNK)))],
    )


def _bd(y, m0):
    z = jnp.zeros_like(y)
    return jnp.concatenate([jnp.where(m0, y, z), jnp.where(m0, z, y)], axis=0)


def _wkv_prepare_chunks(chunks):
    C = CHUNK
    zero = jnp.zeros((C, LANE), F32)
    n = range(len(chunks))
    cs = [ch[6] for ch in chunks]
    m0 = [c["m0"] for c in cs]
    lw = [ch[1] for ch in chunks]
    v = [ch[3] for ch in chunks]
    lw_hi = [x.astype(BF16).astype(F32) for x in lw]
    lw_lo = [lw[i] - lw_hi[i] for i in n]
    cc = [_dot(cs[i]["tri"], jnp.concatenate([lw_hi[i], lw_lo[i]], axis=1)) for i in n]
    tcol = [_dot_tn(jnp.concatenate([lw_hi[i], lw_lo[i]], axis=0), cs[i]["ones"]) for i in n]
    rt, at, bks, pw_lhs, pw_rhs = [], [], [], [], []
    for i in n:
        r, _, k, _, a, b = chunks[i][:6]
        cum = cc[i][:, :LANE] + cc[i][:, LANE:]
        total = cum[0:1] if chunks[i][7] else cum[C - 1:C]
        p_inv = jnp.exp(-cum)
        p_sc = jnp.exp(total - cum)
        rt.append(r * jnp.exp(cum))
        at.append(a * jnp.exp(cum - lw[i]))
        bks.append(jnp.concatenate([b * p_sc, k * p_sc], axis=0))
        pw_lhs.append(jnp.concatenate([at[i], rt[i]], axis=0))
        pw_rhs.append(jnp.concatenate([_bd(b * p_inv, m0[i]), _bd(k * p_inv, m0[i])], axis=0))
    pw = [_dot_nt(pw_lhs[i], pw_rhs[i]) for i in n]
    ab, akrk, rb, tinv = [], [], [], []
    for i in n:
        ab.append(jnp.where(cs[i]["strict"], pw[i][:C, :LANE], zero))
        akrk.append(jnp.concatenate([jnp.where(cs[i]["strict"], pw[i][:C, LANE:], zero),
                                     jnp.where(cs[i]["incl"], pw[i][C:, LANE:], zero)], axis=0))
        rb.append(jnp.where(cs[i]["incl"], pw[i][C:, :LANE], zero))
        tinv.append(cs[i]["eye"] + jnp.where(cs[i]["levels"][0], ab[i], zero))
    for j in range(1, len(cs[0]["levels"])):
        tl = [_dot(tinv[i], _bd(jnp.where(cs[i]["levels"][j], ab[i], zero), m0[i])) for i in n]
        tinv = [tinv[i] + _dot(tl[i], _bd(tinv[i], m0[i])) for i in n]
    akv = [_dot(akrk[i], _bd(v[i], m0[i])) for i in n]
    wx = [_dot(tinv[i], jnp.concatenate([_bd(at[i], m0[i]), _bd(akv[i][:C], m0[i])], axis=1)) for i in n]
    return [dict(wr=jnp.concatenate([wx[i][:, :LANE], rt[i]], axis=0), x=wx[i][:, LANE:], yk=akv[i][C:], rb=rb[i],
                 bks=bks[i], v=v[i], pcol=jnp.exp(tcol[i])) for i in n]


WKV_PAIRS = RW // LANE


def _wkv_kernel(zf_ref, zb_ref, hpf_ref, hnf_ref, hpb_ref, hnb_ref, *rest):
    prm = dict(zip(RWKV_PARAMS, rest))
    yf_o, yb_o, bonf_o, bonb_o, g_o, h_s = rest[len(RWKV_PARAMS):]

    @pl.when(pl.program_id(1) == 0)
    def _():
        h_s[...] = jnp.zeros_like(h_s)

    C = CHUNK
    nch = TM // C
    tiles = (_rwkv_tile(zf_ref[0], hpf_ref[0, 0], hnf_ref[0, 0], prm, 0),
             _rwkv_tile(zb_ref[0], hpb_ref[0, 0], hnb_ref[0, 0], prm, 1))
    bonus = lambda t: _seg_sum(t["bonus_arg"], prm["ones_bd"][...]) * t["v"]
    fillers = [lambda: g_o.__setitem__(0, _dot(_sigmoid(tiles[0]["gd"]).astype(BF16), prm["g2"][...])),
               lambda: bonf_o.__setitem__(0, bonus(tiles[0])),
               lambda: bonb_o.__setitem__(0, bonus(tiles[1]))]
    dirs =((False, tiles[0], yf_o), (True, tiles[1], yb_o))
    order = {False: list(range(nch)), True: list(range(nch - 1, -1, -1))}
    chains = [(rev, tile, y_o, slice(p * LANE, (p + 1) * LANE)) for rev, tile, y_o in dirs for p in range(WKV_PAIRS)]
    consts = {rev: _wkv_consts(rev) for rev, _, _ in dirs}
    chunks = []
    for rev, tile, _, lanes in chains:
        for c in order[rev]:
            chunks.append(tuple(tile[name][c * C:(c + 1) * C, lanes] for name in SCAN_OPERANDS) + (consts[rev], rev))
    prepared = _wkv_prepare_chunks(chunks)
    m0 = consts[False]["m0"]
    bdm = consts[False]["bdm"]
    nc = range(len(chains))
    states = [h_s[j] for j in nc]
    for step in range(nch):
        ps = [prepared[j * nch + step] for j in nc]
        s1 = [_dot(ps[j]["wr"], states[j]) for j in nc]
        if step < len(fillers):
            fillers[step]()
        u = [s1[j][:C] + ps[j]["x"] for j in nc]
        s2 = [_dot(ps[j]["rb"], _bd(u[j], m0)) for j in nc]
        s3 = [_dot_tn(ps[j]["bks"], jnp.concatenate([u[j], ps[j]["v"]], axis=0)) for j in nc]
        for j, (rev, _, y_o, lanes) in enumerate(chains):
            c = order[rev][step]
            y_o[0, c * C:(c + 1) * C, lanes] = s1[j][C:] + s2[j] + ps[j]["yk"]
        states = [ps[j]["pcol"] * states[j] + jnp.where(bdm, s3[j], jnp.zeros_like(s3[j])) for j in nc]
    for j in nc:
        h_s[j] = states[j]


def _wkv_scan(za, hprev, hnext, p):
    B, T, _ = za.shape
    nt = T // TM
    rev_tile = lambda j: jnp.where(j == 0, 0, nt - j)
    tile = lambda n, idx: pl.BlockSpec((1, TM, n), lambda b, i: (b, idx(i), 0))
    halo = lambda idx: pl.BlockSpec((1, 1, 1, RWKV_COLS), lambda b, i: (b, idx(i), 0, 0))
    full = lambda a: pl.BlockSpec(a.shape, lambda b, i: (0,) * a.ndim)
    fwd = lambda i: i
    params = [p[name] for name in RWKV_PARAMS]
    return pl.pallas_call(
        _wkv_kernel,
        grid=(B, nt),
        in_specs=[tile(RWKV_COLS, fwd), tile(RWKV_COLS, rev_tile), halo(fwd), halo(fwd), halo(rev_tile),
                  halo(rev_tile)] + [full(a) for a in params],
        out_specs=[tile(RW, fwd), tile(RW, rev_tile), tile(RW, fwd), tile(RW, rev_tile), tile(RW, fwd)],
        out_shape=[jax.ShapeDtypeStruct((B, T, RW), F32)] * 5,
        scratch_shapes=[pltpu.VMEM((2 * WKV_PAIRS, LANE, LANE), F32)],
        compiler_params=_cparams(("parallel", "arbitrary")),
        name="wkv_scan",
    )(za, za, hprev, hnext, hprev, hnext, *params)


def _load_vt(v_ref, vt_s):
    for c in range(v_ref.shape[1] // TK):
        vt_s[:, c * TK:(c + 1) * TK] = v_ref[0, c * TK:(c + 1) * TK, :].T.astype(BF16)


def _key_chunks(n_keys, ctx_only):
    chunks = [(0, CTX)]
    if not ctx_only:
        big = min(KEY_BLOCK, n_keys - CTX)
        chunks += [(CTX + j * big, big) for j in range((n_keys - CTX) // big)]
    return chunks


def _attend(streams, k_ref, vt_s, chunks):
    state = [(jnp.full((1, TM), -1e30, F32), jnp.zeros((1, TM), F32),
              jnp.zeros((rows.stop - rows.start, TM), F32)) for _, _, rows in streams]
    subs = [[(start + t, min(TK, size - t)) for t in range(0, size, TK)] for start, size in chunks]

    ns = range(len(streams))

    def score(c, t, j):
        st, sz = subs[c][t]
        q, lanes, _ = streams[j]
        s = _dot_nt(k_ref[0, st:st + sz, lanes], q)
        return s, jnp.max(s, axis=0, keepdims=True)

    cur = [[score(0, t, j) for j in ns] for t in range(len(subs[0]))]
    for c in range(len(chunks)):
        n_next = len(subs[c + 1]) if c + 1 < len(chunks) else 0
        nxt = [[] for _ in range(n_next)]
        new_state = []
        for j in ns:
            m, l, acc = state[j]
            m_new = functools.reduce(jnp.maximum, [row[j][1] for row in cur], m)
            alpha = jnp.exp2(m - m_new)
            new_state.append((m_new, alpha * l, alpha * acc))
        for t, (st, sz) in enumerate(subs[c]):
            for j, (_, _, rows) in enumerate(streams):
                if t < n_next:
                    nxt[t].append(score(c + 1, t, j))
                m_new, l, acc = new_state[j]
                p = jnp.exp2(cur[t][j][0] - m_new)
                new_state[j] = (m_new, l + jnp.sum(p, axis=0, keepdims=True),
                                acc + _dot(vt_s[rows, st:st + sz], p.astype(BF16)))
        for t in range(len(subs[c]), n_next):
            nxt[t] = [score(c + 1, t, j) for j in ns]
        state = new_state
        cur = nxt
    return [(acc, l) for _, l, acc in state]


LAG_SUB = 512
LAG_BLOCK = 4096
SAFE_EXCESS = 64.0


def _attend_lagged(streams, k_ref, vt_s):
    n_keys = k_ref.shape[1]
    ns = range(len(streams))
    m, l, acc = [], [], []
    for q, lanes, rows in streams:
        s = _dot_nt(k_ref[0, 0:CTX, lanes], q)
        mj = jnp.max(s, axis=0, keepdims=True)
        p = jnp.exp2(s - mj)
        m.append(mj)
        l.append(jnp.sum(p, axis=0, keepdims=True))
        acc.append(_dot(vt_s[rows, 0:CTX], p.astype(BF16)))
    score = lambda st, sz, j: _dot_nt(k_ref[0, st:st + sz, streams[j][1]], streams[j][0])
    excess = jnp.zeros((1, TM), F32)
    big = min(LAG_BLOCK, n_keys - CTX)
    for start in range(CTX, n_keys, big):
        subs = [(start + t, min(LAG_SUB, big - t)) for t in range(0, big, LAG_SUB)]
        top = list(m)
        cur = [score(*subs[0], j) for j in ns]
        for t, (st, sz) in enumerate(subs):
            nxt = []
            for j in ns:
                if t + 1 < len(subs):
                    nxt.append(score(*subs[t + 1], j))
                s = cur[j]
                top[j] = jnp.maximum(top[j], jnp.max(s, axis=0, keepdims=True))
                p = jnp.exp2(s - m[j])
                l[j] = l[j] + jnp.sum(p, axis=0, keepdims=True)
                acc[j] = acc[j] + _dot(vt_s[streams[j][2], st:st + sz], p.astype(BF16))
            cur = nxt
        for j in ns:
            excess = jnp.maximum(excess, top[j] - m[j])
            alpha = jnp.exp2(m[j] - top[j])
            m[j], l[j], acc[j] = top[j], alpha * l[j], alpha * acc[j]
    return [(acc[j], l[j]) for j in ns], excess


def _run_attention(i, streams, k_ref, vt_s, finish):
    n_keys = k_ref.shape[1]

    @pl.when(i == 0)
    def _():
        finish(_attend(streams, k_ref, vt_s, _key_chunks(n_keys, True)))

    @pl.when(i > 0)
    def _():
        res, excess = _attend_lagged(streams, k_ref, vt_s)
        finish(res)

        @pl.when(jnp.max(excess) > SAFE_EXCESS)
        def _():
            finish(_attend(streams, k_ref, vt_s, _key_chunks(n_keys, False)))


DIFF_HEADS = 2


def _diff_attn_kernel(lam_init, q_ref, k_ref, v_ref, lamv_ref, g_ref, o_ref, vt_s):
    i = pl.program_id(2)

    @pl.when(i == 0)
    def _():
        _load_vt(v_ref, vt_s)

    lv = lamv_ref[...]
    lam = (jnp.exp(jnp.sum(lv[0:1] * lv[1:2], axis=-1, keepdims=True))
           - jnp.exp(jnp.sum(lv[2:3] * lv[3:4], axis=-1, keepdims=True)) + lam_init)
    lane = lax.broadcasted_iota(jnp.int32, (TM, LANE), 1)
    zq = jnp.zeros((TM, LANE), F32)
    heads = [slice(h * LANE, (h + 1) * LANE) for h in range(DIFF_HEADS)]
    streams = []
    for lanes in heads:
        q = q_ref[0, :, lanes] * (DIFF_HD ** -0.5 * LOG2E)
        streams += [(jnp.where(lane < DIFF_HD, q, zq), lanes, lanes), (jnp.where(lane < DIFF_HD, zq, q), lanes, lanes)]

    def finish(res):
        for h, lanes in enumerate(heads):
            (a1, l1), (a2, l2) = res[2 * h], res[2 * h + 1]
            o = a1 / l1 - lam * (a2 / l2)
            ms = jnp.mean(o * o, axis=0, keepdims=True)
            o = o * lax.rsqrt(ms + NORM_EPS)
            o_ref[0, :, lanes] = o.T * g_ref[...] * (1.0 - lam_init)

    _run_attention(i, streams, k_ref, vt_s, finish)


def _diff_attention(zb, lam_vecs, subln_g, layer_idx):
    B, T, _ = zb.shape
    nt = T // TM
    lam_init = 0.8 - 0.6 * math.exp(-0.3 * layer_idx)
    width = DIFF_HEADS * LANE
    groups = H_B // DIFF_HEADS
    return pl.pallas_call(
        functools.partial(_diff_attn_kernel, lam_init),
        grid=(B, groups, nt),
        in_specs=[
            pl.BlockSpec((1, TM, width), lambda b, h, i: (b, i, h)),
            pl.BlockSpec((1, T, width), lambda b, h, i: (b, 0, groups + h)),
            pl.BlockSpec((1, T, width), lambda b, h, i: (b, 0, 2 * groups + h)),
            pl.BlockSpec((4, DIFF_HD), lambda b, h, i: (0, 0)),
            pl.BlockSpec((1, LANE), lambda b, h, i: (0, 0)),
        ],
        out_specs=pl.BlockSpec((1, TM, width), lambda b, h, i: (b, i, h)),
        out_shape=jax.ShapeDtypeStruct((B, T, H_B * LANE), F32),
        scratch_shapes=[pltpu.VMEM((width, T), BF16)],
        compiler_params=_cparams(("parallel", "parallel", "arbitrary")),
        name="diff_attn",
    )(zb, zb, zb, lam_vecs, subln_g)


def _mla_proj_kernel(z_ref, qg_ref, kvg_ref, wq_ref, wqr_ref, wk_ref, wv_ref, cq_ref, sq_ref, ck_ref, sk_ref,
                     q_o, k_o, v_o):
    z = z_ref[0]
    cq = z[:, 0:Q_LORA]
    ckv = z[:, Q_LORA:Q_LORA + KV_LORA]
    kr = z[:, 5 * LANE:6 * LANE] * ck_ref[...] + z[:, 6 * LANE:7 * LANE] * sk_ref[...]
    rms = lambda t: t * lax.rsqrt(jnp.mean(t * t, axis=-1, keepdims=True) + NORM_EPS)
    cqn = (rms(cq) * qg_ref[...]).astype(BF16)
    ckvn = rms(ckv) * kvg_ref[...]
    q = _dot(cqn, wq_ref[...])
    qr = _dot(cqn, wqr_ref[...])
    cos = cq_ref[...]
    sin = sq_ref[...]
    scale = (MLA_NOPE + MLA_ROPE) ** -0.5 * LOG2E
    for h in range(H_C):
        sl = slice(h * LANE, (h + 1) * LANE)
        q_o[0, :, sl] = (q[:, sl] * cos + qr[:, sl] * sin) * scale
    kin = jnp.concatenate([ckvn, kr], axis=1).astype(BF16)
    k_o[0] = _dot(kin, wk_ref[...])
    v_o[0] = _dot(ckvn.astype(BF16), wv_ref[...])


def _mla_project(zc, p):
    B, T, _ = zc.shape
    full = lambda a: pl.BlockSpec(a.shape, lambda b, i: (0,) * a.ndim)
    tab = pl.BlockSpec((TM, LANE), lambda b, i: (i, 0))
    ws = [p["q_g"], p["kv_g"], p["w_uq"], p["w_uq_rot"], p["w_uk"], p["w_uv"]]
    return pl.pallas_call(
        _mla_proj_kernel,
        grid=(B, T // TM),
        in_specs=[pl.BlockSpec((1, TM, MLA_PAD), lambda b, i: (b, i, 0))] + [full(a) for a in ws] + [tab] * 4,
        out_specs=[
            pl.BlockSpec((1, TM, H_C * LANE), lambda b, i: (b, i, 0)),
            pl.BlockSpec((1, TM, H_C * LANE), lambda b, i: (b, i, 0)),
            pl.BlockSpec((1, TM, H_C * MLA_V), lambda b, i: (b, i, 0)),
        ],
        out_shape=[
            jax.ShapeDtypeStruct((B, T, H_C * LANE), F32),
            jax.ShapeDtypeStruct((B, T, H_C * LANE), F32),
            jax.ShapeDtypeStruct((B, T, H_C * MLA_V), F32),
        ],
        compiler_params=_cparams(("parallel", "parallel")),
        name="mla_project",
    )(zc, *ws, p["cos_q"], p["sin_q"], p["cos_k"], p["sin_k"])


MLA_HEADS = 4


def _mla_attn_kernel(q_ref, k_ref, v_ref, o_ref, vt_s):
    i = pl.program_id(2)

    @pl.when(i == 0)
    def _():
        _load_vt(v_ref, vt_s)

    streams = [(q_ref[0, :, h * LANE:(h + 1) * LANE], slice(h * LANE, (h + 1) * LANE),
                slice(h * MLA_V, (h + 1) * MLA_V)) for h in range(MLA_HEADS)]

    def finish(res):
        for p in range(MLA_HEADS // 2):
            (a0, l0), (a1, l1) = res[2 * p], res[2 * p + 1]
            o_ref[0, :, p * LANE:(p + 1) * LANE] = jnp.concatenate([a0 / l0, a1 / l1], axis=0).T

    _run_attention(i, streams, k_ref, vt_s, finish)


def _mla_attention(q, k, v):
    B, T, _ = q.shape
    nt = T // TM
    return pl.pallas_call(
        _mla_attn_kernel,
        grid=(B, H_C // MLA_HEADS, nt),
        in_specs=[
            pl.BlockSpec((1, TM, MLA_HEADS * LANE), lambda b, p, i: (b, i, p)),
            pl.BlockSpec((1, T, MLA_HEADS * LANE), lambda b, p, i: (b, 0, p)),
            pl.BlockSpec((1, T, MLA_HEADS * MLA_V), lambda b, p, i: (b, 0, p)),
        ],
        out_specs=pl.BlockSpec((1, TM, MLA_HEADS * MLA_V), lambda b, p, i: (b, i, p)),
        out_shape=jax.ShapeDtypeStruct((B, T, H_C * MLA_V), F32),
        scratch_shapes=[pltpu.VMEM((MLA_HEADS * MLA_V, T), BF16)],
        compiler_params=_cparams(("parallel", "parallel", "arbitrary")),
        name="mla_attn",
    )(q, k, v)


def _merge_kernel(x_ref, yf_ref, yb_ref, bonf_ref, bonb_ref, g_ref, yd_ref, yc_ref, zg_ref, g1_ref, lnw_ref,
                  lnb_ref, ones_ref, wb_ref, wo_ref, o_ref):
    ones_bd = ones_ref[...]
    y = yf_ref[0] + yb_ref[0]
    mean = _seg_sum(y, ones_bd) * (1.0 / N_A)
    yc = y - mean
    var = _seg_sum(yc * yc, ones_bd) * (1.0 / N_A)
    bonus = bonf_ref[0] + bonb_ref[0]
    ya = (yc * lax.rsqrt(var + GN_EPS) * lnw_ref[...] + lnb_ref[...] + bonus) * g_ref[0]
    acc = None
    for j, yj in enumerate((ya, yd_ref[0], yc_ref[0])):
        gate = _sigmoid(zg_ref[0, :, j * D:(j + 1) * D])
        t = gate * _dot(yj.astype(BF16), wb_ref[j])
        acc = t if acc is None else acc + t
    o_ref[0] = x_ref[0] + g1_ref[0, 0] * _dot(acc.astype(BF16), wo_ref[...])


def _merge(x_all, yf, yb, bonf, bonb, g, yd, yc, zg, g1, lnw, lnb, ones_bd, wb, wo, latent_only):
    B, T, _ = x_all.shape
    skip = 1 if latent_only else 0
    row = lambda n: pl.BlockSpec((1, TM, n), lambda b, i: (b, i + skip, 0))
    full = lambda a: pl.BlockSpec(a.shape, lambda b, i: (0,) * a.ndim)
    return pl.pallas_call(
        _merge_kernel,
        grid=(B, T // TM - skip),
        in_specs=[row(D)] + [row(RW)] * 7 + [row(3 * D),
                  pl.BlockSpec((1, 1, 1, D), lambda b, i: (b, _is_ctx(i + skip), 0, 0)),
                  full(lnw), full(lnb), full(ones_bd), full(wb), full(wo)],
        out_specs=pl.BlockSpec((1, TM, D), lambda b, i: (b, i, 0)),
        out_shape=jax.ShapeDtypeStruct((B, T - skip * TM, D), F32),
        compiler_params=_cparams(("parallel", "parallel")),
        name="merge",
    )(x_all, yf, yb, bonf, bonb, g, yd, yc, zg, g1, lnw, lnb, ones_bd, wb, wo)


def _ffn_kernel(final, x_ref, g_ref, mod_ref, wg_ref, wu_ref, wd_ref, fg_ref, o_ref):
    x = x_ref[0]
    h = _norm_mod(x, g_ref[...], mod_ref[0, 0, 0:1, :], mod_ref[0, 0, 1:2, :]).astype(BF16)
    gate = _dot(h, wg_ref[...])
    up = _dot(h, wu_ref[...])
    act = (gate * _sigmoid(gate) * up).astype(BF16)
    out = x + mod_ref[0, 0, 2:3, :] * _dot(act, wd_ref[...])
    if final:
        ms = jnp.mean(out * out, axis=-1, keepdims=True)
        out = out * lax.rsqrt(ms + NORM_EPS) * fg_ref[...]
    o_ref[0] = out


def _ffn(x_all, g, mod, wg, wu, wd, fg, final):
    B, T, _ = x_all.shape
    full = lambda a: pl.BlockSpec(a.shape, lambda b, i: (0,) * a.ndim, pipeline_mode=pl.Buffered(1))
    return pl.pallas_call(
        functools.partial(_ffn_kernel, final),
        grid=(B, T // TM),
        in_specs=[
            pl.BlockSpec((1, TM, D), lambda b, i: (b, i, 0)),
            pl.BlockSpec((1, D), lambda b, i: (0, 0)),
            pl.BlockSpec((1, 1, 3, D), lambda b, i: (b, 0 if final else _is_ctx(i), 0, 0)),
            full(wg), full(wu), full(wd),
            pl.BlockSpec((1, D), lambda b, i: (0, 0)),
        ],
        out_specs=pl.BlockSpec((1, TM, D), lambda b, i: (b, i, 0)),
        out_shape=jax.ShapeDtypeStruct((B, T, D), F32),
        compiler_params=_cparams(("parallel", "parallel")),
        name="ffn",
    )(x_all, g, mod, wg, wu, wd, fg)


def _rot_cols(w, half):
    d, n = w.shape
    w4 = w.reshape(d, n // (2 * half), 2, half)
    return jnp.concatenate([-w4[:, :, 1:2], w4[:, :, 0:1]], axis=2).reshape(d, n)


def _rope_tables(n_lat):
    rows = n_lat // GRID_W
    row = jnp.repeat(jnp.arange(rows, dtype=F32), GRID_W)
    col = jnp.tile(jnp.arange(GRID_W, dtype=F32), rows)

    def tables(rot_dim):
        n_freq = rot_dim // 4
        inv_freq = ROPE_BASE ** (-jnp.arange(n_freq, dtype=F32) / n_freq)
        ang = jnp.concatenate([row[:, None] * inv_freq, col[:, None] * inv_freq], axis=-1)
        cos = jnp.concatenate([jnp.ones((CTX, rot_dim // 2), F32), jnp.cos(ang)], axis=0)
        sin = jnp.concatenate([jnp.zeros((CTX, rot_dim // 2), F32), jnp.sin(ang)], axis=0)
        return cos, sin

    T = CTX + n_lat
    cos_b, sin_b = tables(DIFF_HD)
    cos_c, sin_c = tables(MLA_ROPE)
    one = lambda n: jnp.ones((T, n), F32)
    zero = lambda n: jnp.zeros((T, n), F32)
    return dict(
        cos_b=jnp.tile(cos_b, (1, 4)), sin_b=jnp.tile(sin_b, (1, 4)),
        cos_q=jnp.concatenate([one(MLA_NOPE), cos_c, cos_c, one(32)], axis=1),
        sin_q=jnp.concatenate([zero(MLA_NOPE), sin_c, sin_c, zero(32)], axis=1),
        cos_k=jnp.concatenate([cos_c, cos_c, one(96)], axis=1),
        sin_k=jnp.concatenate([sin_c, sin_c, zero(96)], axis=1),
    )


def _pad_heads(w, per_head, n_heads):
    d = w.shape[0]
    w3 = w.reshape(d, n_heads, per_head)
    return jnp.pad(w3, ((0, 0), (0, 0), (0, LANE - per_head))).reshape(d, n_heads * LANE)


def _layer_weights(l, ws):
    (w_in, mu, w0, w2, a0, a2, g2, k_k, k_a, r_k, ln_w, ln_b, q_g, kv_g, w_uq, w_uk, w_uv, w_branch, w_out,
     wg, wu, wd) = [w[l] for w in ws]
    o1, o2, o3 = RWKV_COLS, RWKV_COLS + DIFF_COLS, RWKV_COLS + DIFF_COLS + MLA_COLS
    w_a, w_b, w_c, w_g = w_in[:, :o1], w_in[:, o1:o2], w_in[:, o2:o3], w_in[:, o3:]
    w_kr = w_c[:, Q_LORA + KV_LORA:]
    w_c_ext = jnp.concatenate([w_c, jnp.zeros((D, 96), F32), _rot_cols(w_kr, MLA_ROPE // 2),
                               jnp.zeros((D, 96), F32)], axis=1)
    zpad = jnp.zeros((2, 64, RW), F32)
    seg = jnp.arange(RW) // N_A
    w_uq_p = _pad_heads(w_uq, MLA_NOPE + MLA_ROPE, H_C)
    w_uq_rope = w_uq.reshape(Q_LORA, H_C, MLA_NOPE + MLA_ROPE)[:, :, MLA_NOPE:].reshape(Q_LORA, H_C * MLA_ROPE)
    w_uq_rot = _rot_cols(w_uq_rope, MLA_ROPE // 2).reshape(Q_LORA, H_C, MLA_ROPE)
    w_uq_rot_p = jnp.pad(w_uq_rot, ((0, 0), (0, 0), (MLA_NOPE, LANE - MLA_NOPE - MLA_ROPE))).reshape(Q_LORA, H_C * LANE)
    place = jnp.zeros((LANE, H_C, LANE), F32).at[
        jnp.arange(MLA_ROPE)[:, None], jnp.arange(H_C)[None, :], MLA_NOPE + jnp.arange(MLA_ROPE)[:, None]].set(1.0)
    w_uk_p = jnp.concatenate([_pad_heads(w_uk, MLA_NOPE, H_C), place.reshape(LANE, H_C * LANE)], axis=0)
    return dict(
        w_in=jnp.concatenate([w_a, w_c_ext, w_g, w_b], axis=1).astype(BF16),
        rwkv=dict(
            mu=mu, w0=w0, a0=a0,
            w2=jnp.stack([jnp.concatenate([w2[0], zpad[0]], 0), jnp.concatenate([zpad[1], w2[1]], 0)]).astype(BF16),
            a2=jnp.stack([jnp.concatenate([a2[0], zpad[0]], 0), jnp.concatenate([zpad[1], a2[1]], 0)]).astype(BF16),
            g2=g2.astype(BF16), k_k=k_k[None], k_a=k_a[None], r_k=r_k.reshape(1, RW),
            ones_bd=(seg[:, None] == seg[None, :]).astype(BF16)),
        ln_w=ln_w[None], ln_b=ln_b[None],
        mla=dict(q_g=q_g[None], kv_g=kv_g[None], w_uq=w_uq_p.astype(BF16), w_uq_rot=w_uq_rot_p.astype(BF16),
                 w_uk=w_uk_p.astype(BF16), w_uv=w_uv.astype(BF16)),
        w_branch=w_branch.astype(BF16), w_out=w_out.astype(BF16),
        wg=wg.astype(BF16), wu=wu.astype(BF16), wd=wd.astype(BF16),
    )


def _mod_rows(m, idx, B):
    lat = m[:B][:, idx]
    ctx = jnp.broadcast_to(m[B][idx][None], lat.shape)
    return jnp.stack([lat, ctx], axis=1)


def kernel(x, c, ctx, c_ctx, ada_w, ada_b, norm1_g, norm2_g, w_in, rwkv_shift_mu, rwkv_w0, rwkv_w2, rwkv_a0,
           rwkv_a2, rwkv_g2, rwkv_k_k, rwkv_k_a, rwkv_r_k, rwkv_ln_w, rwkv_ln_b, diff_lambda, diff_subln_g,
           mla_q_norm_g, mla_kv_norm_g, mla_w_uq, mla_w_uk, mla_w_uv, w_branch, w_out, ffn_w_gate, ffn_w_up,
           ffn_w_down, final_norm_g):
    B, S, _ = x.shape
    L = ada_w.shape[0]
    T = CTX + S
    nt = T // TM
    x_all = jnp.concatenate([ctx, x], axis=1)
    c8 = jnp.concatenate([c, c_ctx[None], jnp.zeros((8 - B - 1, D), F32)], axis=0)
    mod = _modulation(c8, ada_w, ada_b).reshape(L, 8, 6, D)
    tabs = _rope_tables(S)
    ws = (w_in, rwkv_shift_mu, rwkv_w0, rwkv_w2, rwkv_a0, rwkv_a2, rwkv_g2, rwkv_k_k, rwkv_k_a, rwkv_r_k,
          rwkv_ln_w, rwkv_ln_b, mla_q_norm_g, mla_kv_norm_g, mla_w_uq, mla_w_uk, mla_w_uv, w_branch, w_out,
          ffn_w_gate, ffn_w_up, ffn_w_down)
    for l in range(L):
        p = _layer_weights(l, ws)
        mod1 = _mod_rows(mod[l], jnp.array([0, 1]), B)
        g1 = _mod_rows(mod[l], jnp.array([2]), B)
        mod2 = _mod_rows(mod[l], jnp.array([3, 4, 5]), B)
        n1 = norm1_g[l][None]

        za, zc, zg, zb, first8, last8 = _inproj(x_all, n1, mod1, p["w_in"], tabs["cos_b"], tabs["sin_b"])

        zeros = lambda n: jnp.zeros((B, n, RWKV_COLS), F32)
        last_rows, first_rows = last8[:, :, HALO - 1], first8[:, :, 0]
        hprev = jnp.concatenate([zeros(2), last_rows[:, 1:nt - 1]], axis=1)[:, :, None]
        hnext = jnp.concatenate([zeros(1), first_rows[:, 2:nt], zeros(1)], axis=1)[:, :, None]
        yf, yb, bonf, bonb, g = _wkv_scan(za, hprev, hnext, p["rwkv"])

        yd = _diff_attention(zb, diff_lambda[l], diff_subln_g[l][None], l)

        q, k, vm = _mla_project(zc, dict(p["mla"], cos_q=tabs["cos_q"], sin_q=tabs["sin_q"],
                                         cos_k=tabs["cos_k"], sin_k=tabs["sin_k"]))
        yc = _mla_attention(q, k, vm)

        last = l == L - 1
        x_all = _merge(x_all, yf, yb, bonf, bonb, g, yd, yc, zg, g1, p["ln_w"], p["ln_b"], p["rwkv"]["ones_bd"],
                       p["w_branch"], p["w_out"], last)
        x_all = _ffn(x_all, norm2_g[l][None], mod2, p["wg"], p["wu"], p["wd"], final_norm_g[None], last)
    return x_all
```
